```python
import math
import jax, jax.numpy as jnp
from jax import lax
import numpy as np

D_MODEL = 1024
BATCH = 8
SEQ = 4096
DEPTH = 4

CHUNK = 64
N_A_LAYERS = DEPTH // 2
N_B_LAYERS = DEPTH - N_A_LAYERS
D_FF = 2816
EPS = 1e-6

SSM_EXPAND = 2
D_INNER = SSM_EXPAND * D_MODEL
SSM_HEAD_DIM = 64
SSM_HEADS = D_INNER // SSM_HEAD_DIM
SSM_GROUPS = 8
SSM_HEADS_PER_GROUP = SSM_HEADS // SSM_GROUPS
D_STATE = 128
CONV_W = 4
CONV_DIM = D_INNER + 2 * SSM_GROUPS * D_STATE
D_IN_PROJ = 2 * D_INNER + 2 * SSM_GROUPS * D_STATE + SSM_HEADS
SSD_CHUNK = CHUNK
DT_MIN = 1e-3
DT_MAX = 1e-1

ATT_HEADS = 16
ATT_HEAD_DIM = 64
ATT_DIM = ATT_HEADS * ATT_HEAD_DIM
LEFT_CHUNKS = 8
BAND = (LEFT_CHUNKS + 1) * CHUNK
PAD_LEN = LEFT_CHUNKS * CHUNK
MAX_REL = 128
N_REL = 2 * MAX_REL + 1

kernel_name = "hybrid_ssd_shared_kv_chunk_attn_macaron"


def rms_norm(x, g):
    xf = x.astype(jnp.float32)
    y = xf * lax.rsqrt(jnp.mean(xf * xf, axis=-1, keepdims=True) + EPS)
    return (y * g.astype(jnp.float32)).astype(x.dtype)


def swiglu(h, w_gate, w_up, w_down):
    return (jax.nn.silu(h @ w_gate) * (h @ w_up)) @ w_down


def causal_depthwise_conv(x, w, b):
    c = x.shape[-1]
    y = lax.conv_general_dilated(
        x, w.astype(x.dtype)[:, None, :], window_strides=(1,),
        padding=[(CONV_W - 1, 0)], dimension_numbers=("NWC", "WIO", "NWC"),
        feature_group_count=c)
    return y + b.astype(x.dtype)


def ssd_scan(X, A, Bm, Cm):
    b, S = X.shape[0], X.shape[1]
    nc, L = S // SSD_CHUNK, SSD_CHUNK
    G, R = SSM_GROUPS, SSM_HEADS_PER_GROUP
    X = X.reshape(b, nc, L, G, R, SSM_HEAD_DIM)
    A = A.reshape(b, nc, L, G, R).transpose(0, 3, 4, 1, 2)
    Bm = Bm.reshape(b, nc, L, G, D_STATE)
    Cm = Cm.reshape(b, nc, L, G, D_STATE)
    A_cs = jnp.cumsum(A, axis=-1)
    tril = jnp.tril(jnp.ones((L, L), dtype=bool))
    seg = A_cs[..., :, None] - A_cs[..., None, :]
    Lmat = jnp.exp(jnp.where(tril, seg, -jnp.inf))
    CB = jnp.einsum('bclgn,bcsgn->bgcls', Cm, Bm)
    W = CB[:, :, None] * Lmat
    Y_diag = jnp.einsum('bgrcls,bcsgrp->bclgrp', W, X)
    decay_states = jnp.exp(A_cs[..., -1:] - A_cs).transpose(0, 3, 4, 1, 2)
    states = jnp.einsum('bclgn,bclgrp->bcgrpn', Bm, X * decay_states[..., None])
    chunk_decay = jnp.exp(A_cs[..., -1])

    def step(h, inp):
        s_c, d_c = inp
        return h * d_c[..., None, None] + s_c, h

    h0 = jnp.zeros((b, G, R, SSM_HEAD_DIM, D_STATE), jnp.float32)
    _, prev = lax.scan(step, h0, (states.transpose(1, 0, 2, 3, 4, 5),
                                  chunk_decay.transpose(3, 0, 1, 2)))
    prev = prev.transpose(1, 0, 2, 3, 4, 5)
    out_decay = jnp.exp(A_cs).transpose(0, 3, 4, 1, 2)
    Y_off = jnp.einsum('bclgn,bcgrpn->bclgrp', Cm, prev) * out_decay[..., None]
    return (Y_diag + Y_off).reshape(b, S, SSM_HEADS, SSM_HEAD_DIM)


def mamba2_mixer(h, in_proj, conv_w, conv_b, dt_bias, A_log, D_skip, out_norm, out_proj):
    b, S, _ = h.shape
    zxbcdt = h @ in_proj
    z = zxbcdt[..., :D_INNER]
    xBC = zxbcdt[..., D_INNER:D_INNER + CONV_DIM]
    dt = zxbcdt[..., D_INNER + CONV_DIM:]
    xBC = jax.nn.silu(causal_depthwise_conv(xBC, conv_w, conv_b)).astype(jnp.float32)
    xs = xBC[..., :D_INNER].reshape(b, S, SSM_HEADS, SSM_HEAD_DIM)
    Bm = xBC[..., D_INNER:D_INNER + SSM_GROUPS * D_STATE].reshape(b, S, SSM_GROUPS, D_STATE)
    Cm = xBC[..., D_INNER + SSM_GROUPS * D_STATE:].reshape(b, S, SSM_GROUPS, D_STATE)
    dt = jax.nn.softplus(dt.astype(jnp.float32) + dt_bias.astype(jnp.float32))
    A = -jnp.exp(A_log.astype(jnp.float32))
    y = ssd_scan(xs * dt[..., None], dt * A, Bm, Cm)
    y = y + D_skip.astype(jnp.float32)[:, None] * xs
    gated = y.reshape(b, S, D_INNER) * jax.nn.silu(z.astype(jnp.float32))
    gg = gated.reshape(b, S, SSM_GROUPS, D_INNER // SSM_GROUPS)
    gg = gg * lax.rsqrt(jnp.mean(gg * gg, axis=-1, keepdims=True) + EPS)
    y = (gg.reshape(b, S, D_INNER) * out_norm.astype(jnp.float32)).astype(h.dtype)
    return y @ out_proj


def shared_kv(x, kv_norm, w_kv, k_norm):
    b, S, _ = x.shape
    kv = rms_norm(x, kv_norm) @ w_kv
    k = rms_norm(kv[..., :ATT_DIM].reshape(b, S, ATT_HEADS, ATT_HEAD_DIM), k_norm)
    v = kv[..., ATT_DIM:].reshape(b, S, ATT_HEADS, ATT_HEAD_DIM)
    pad = ((0, 0), (PAD_LEN, 0), (0, 0), (0, 0))
    return jnp.pad(k, pad), jnp.pad(v, pad)


def chunk_attention(h, k_pad, v_pad, w_q, q_norm, rel_bias, w_o):
    b, S, _ = h.shape
    nc = S // CHUNK
    q = rms_norm((h @ w_q).reshape(b, S, ATT_HEADS, ATT_HEAD_DIM), q_norm)
    q_chunks = q.reshape(b, nc, CHUNK, ATT_HEADS, ATT_HEAD_DIM).transpose(1, 0, 2, 3, 4)
    rel = (np.arange(BAND) - PAD_LEN)[None, :] - np.arange(CHUNK)[:, None]
    rel_idx = np.clip(rel, -MAX_REL, MAX_REL) + MAX_REL
    bias = rel_bias.astype(jnp.float32)[:, rel_idx]
    scale = 1.0 / math.sqrt(ATT_HEAD_DIM)
    band_off = jnp.arange(BAND, dtype=jnp.int32) - PAD_LEN

    def attend(args):
        q_c, c = args
        start = c * CHUNK
        k_b = lax.dynamic_slice_in_dim(k_pad, start, BAND, axis=1)
        v_b = lax.dynamic_slice_in_dim(v_pad, start, BAND, axis=1)
        s = jnp.einsum('bqhd,bkhd->bhqk', q_c, k_b,
                       preferred_element_type=jnp.float32) * scale + bias
        valid = (start + band_off) >= 0
        s = jnp.where(valid[None, None, None, :], s, -jnp.inf)
        p = jax.nn.softmax(s, axis=-1).astype(v_b.dtype)
        return jnp.einsum('bhqk,bkhd->bqhd', p, v_b)

    o = lax.map(attend, (q_chunks, jnp.arange(nc, dtype=jnp.int32)))
    o = o.transpose(1, 0, 2, 3, 4).reshape(b, S, ATT_DIM)
    return o @ w_o


def setup_inputs(seed: int = 0) -> dict:
    key = jax.random.key(seed)
    ks = jax.random.split(key, 32)
    f32 = jnp.float32

    def nrm(k, shape, scale):
        return jax.random.normal(k, shape, f32) * scale

    def gain(k, shape):
        return 1.0 + 0.02 * jax.random.normal(k, shape, f32)

    NA, NB = N_A_LAYERS, N_B_LAYERS
    u = jax.random.uniform(ks[14], (NA, SSM_HEADS), f32)
    dt0 = jnp.exp(u * (math.log(DT_MAX) - math.log(DT_MIN)) + math.log(DT_MIN))
    dt_bias = dt0 + jnp.log(-jnp.expm1(-dt0))
    return {
        "x": jax.random.normal(ks[0], (BATCH, SEQ, D_MODEL), f32),
        "ffn1_norm": gain(ks[1], (DEPTH, D_MODEL)),
        "ffn1_w_gate": nrm(ks[2], (DEPTH, D_MODEL, D_FF), D_MODEL ** -0.5),
        "ffn1_w_up": nrm(ks[3], (DEPTH, D_MODEL, D_FF), D_MODEL ** -0.5),
        "ffn1_w_down": nrm(ks[4], (DEPTH, D_FF, D_MODEL), D_FF ** -0.5),
        "ffn2_norm": gain(ks[5], (DEPTH, D_MODEL)),
        "ffn2_w_gate": nrm(ks[6], (DEPTH, D_MODEL, D_FF), D_MODEL ** -0.5),
        "ffn2_w_up": nrm(ks[7], (DEPTH, D_MODEL, D_FF), D_MODEL ** -0.5),
        "ffn2_w_down": nrm(ks[8], (DEPTH, D_FF, D_MODEL), D_FF ** -0.5),
        "ssm_norm": gain(ks[9], (NA, D_MODEL)),
        "ssm_in_proj": nrm(ks[10], (NA, D_MODEL, D_IN_PROJ), D_MODEL ** -0.5),
        "ssm_conv_w": nrm(ks[11], (NA, CONV_W, CONV_DIM), CONV_W ** -0.5),
        "ssm_conv_b": nrm(ks[12], (NA, CONV_DIM), 0.02),
        "ssm_dt_bias": dt_bias,
        "ssm_A_log": jnp.log(jax.random.uniform(ks[13], (NA, SSM_HEADS), f32, 1.0, 16.0)),
        "ssm_D": gain(ks[15], (NA, SSM_HEADS)),
        "ssm_out_norm": gain(ks[16], (NA, D_INNER)),
        "ssm_out_proj": nrm(ks[17], (NA, D_INNER, D_MODEL), D_INNER ** -0.5),
        "kv_norm": gain(ks[18], (D_MODEL,)),
        "w_kv": nrm(ks[19], (D_MODEL, 2 * ATT_DIM), D_MODEL ** -0.5),
        "k_norm": gain(ks[20], (ATT_HEAD_DIM,)),
        "att_norm": gain(ks[21], (NB, D_MODEL)),
        "att_w_q": nrm(ks[22], (NB, D_MODEL, ATT_DIM), D_MODEL ** -0.5),
        "att_q_norm": gain(ks[23], (NB, ATT_HEAD_DIM)),
        "att_rel_bias": nrm(ks[24], (NB, ATT_HEADS, N_REL), 0.1),
        "att_w_o": nrm(ks[25], (NB, ATT_DIM, D_MODEL), ATT_DIM ** -0.5),
    }


def reference(x, ffn1_norm, ffn1_w_gate, ffn1_w_up, ffn1_w_down,
              ffn2_norm, ffn2_w_gate, ffn2_w_up, ffn2_w_down,
              ssm_norm, ssm_in_proj, ssm_conv_w, ssm_conv_b, ssm_dt_bias,
              ssm_A_log, ssm_D, ssm_out_norm, ssm_out_proj,
              kv_norm, w_kv, k_norm,
              att_norm, att_w_q, att_q_norm, att_rel_bias, att_w_o):
    k_pad, v_pad = None, None
    for l in range(DEPTH):
        x = x + 0.5 * swiglu(rms_norm(x, ffn1_norm[l]), ffn1_w_gate[l], ffn1_w_up[l], ffn1_w_down[l])
        if l < N_A_LAYERS:
            x = x + mamba2_mixer(rms_norm(x, ssm_norm[l]), ssm_in_proj[l], ssm_conv_w[l],
                                 ssm_conv_b[l], ssm_dt_bias[l], ssm_A_log[l], ssm_D[l],
                                 ssm_out_norm[l], ssm_out_proj[l])
        else:
            if l == N_A_LAYERS:
                k_pad, v_pad = shared_kv(x, kv_norm, w_kv, k_norm)
            j = l - N_A_LAYERS
            x = x + chunk_attention(rms_norm(x, att_norm[j]), k_pad, v_pad, att_w_q[j],
                                    att_q_norm[j], att_rel_bias[j], att_w_o[j])
        x = x + 0.5 * swiglu(rms_norm(x, ffn2_norm[l]), ffn2_w_gate[l], ffn2_w_up[l], ffn2_w_down[l])
    return x
```

```python
import functools
import math

import jax
import jax.numpy as jnp
from jax import lax
from jax.experimental import pallas as pl
from jax.experimental.pallas import tpu as pltpu

F32 = jnp.float32
BF16 = jnp.bfloat16

D_MODEL = 1024
D_FF = 2816
EPS = 1e-6

D_INNER = 2048
HEAD_DIM = 64
SSM_HEADS = 32
SSM_GROUPS = 8
D_STATE = 128
CONV_W = 4
CONV_DIM = D_INNER + 2 * SSM_GROUPS * D_STATE
GROUP_W = D_INNER // SSM_GROUPS
CHUNK = 64

ATT_HEADS = 16
ATT_DIM = 1024
LEFT_CHUNKS = 8
BAND = (LEFT_CHUNKS + 1) * CHUNK
PAD_LEN = LEFT_CHUNKS * CHUNK
MAX_REL = 128
TOEP_W = 640

LANES = 128
VMEM_LIMIT = 56 * 1024 * 1024

FFN_ROWS = 512
SSM_ROWS = 256
PROJ_ROWS = 512
ATT_ROWS = 256


def _resident(shape):
    nd = len(shape)
    return pl.BlockSpec(shape, lambda *_: (0,) * nd, pipeline_mode=pl.Buffered(1))


def _params(*sem):
    return pltpu.CompilerParams(dimension_semantics=sem, vmem_limit_bytes=VMEM_LIMIT)


def _rms(x, g):
    return x * lax.rsqrt(jnp.mean(x * x, axis=-1, keepdims=True) + EPS) * g


def _dot(a, b):
    return jnp.dot(a, b, preferred_element_type=F32)


def _dot_nt(a, b):
    return lax.dot_general(a, b, (((1,), (1,)), ((), ())), preferred_element_type=F32)


def _dot_tn(a, b):
    return lax.dot_general(a, b, (((0,), (0,)), ((), ())), preferred_element_type=F32)


def _split2(v):
    hi = v.astype(BF16)
    lo = (v - hi.astype(F32)).astype(BF16)
    return hi, lo


def _silu(v):
    return v * jax.nn.sigmoid(v)


def _ffn_kernel(x_ref, g_ref, wg_ref, wu_ref, wd_ref, o_ref):
    x = x_ref[...]
    xn = _rms(x, g_ref[...]).astype(BF16)
    gate = _dot(xn, wg_ref[...])
    up = _dot(xn, wu_ref[...])
    h = (_silu(gate) * up).astype(BF16)
    o_ref[...] = x + 0.5 * _dot(h, wd_ref[...])


def _ffn(x2, g, wg, wu, wd):
    t = x2.shape[0]
    tm = min(FFN_ROWS, t)
    return pl.pallas_call(
        _ffn_kernel,
        grid=(t // tm,),
        in_specs=[
            pl.BlockSpec((tm, D_MODEL), lambda i: (i, 0)),
            _resident((1, D_MODEL)),
            _resident((D_MODEL, D_FF)),
            _resident((D_MODEL, D_FF)),
            _resident((D_FF, D_MODEL)),
        ],
        out_specs=pl.BlockSpec((tm, D_MODEL), lambda i: (i, 0)),
        out_shape=jax.ShapeDtypeStruct((t, D_MODEL), F32),
        compiler_params=_params("arbitrary"),
        name="ffn",
    )(x2, g.reshape(1, D_MODEL), wg, wu, wd)


def _ssm_kernel(x_ref, g_ref, wz_ref, wxbc_ref, wdt_ref, cw_ref, cb_ref, dtb_ref,
                alog_ref, alogf_ref, dskip_ref, onorm_ref, wout_ref,
                expand_ref, triu_ref, negmask_ref, o_ref,
                h_scr, tail_scr, xpad_scr, xc_scr, z_scr, dt_scr, y_scr):
    ts = x_ref.shape[1]

    @pl.when(pl.program_id(1) == 0)
    def _():
        h_scr[...] = jnp.zeros_like(h_scr)
        tail_scr[...] = jnp.zeros_like(tail_scr)

    x = x_ref[0]
    xn = _rms(x, g_ref[...]).astype(BF16)
    z_scr[...] = _dot(xn, wz_ref[...])

    xpad_scr[0:8, :] = tail_scr[...]
    xpad_scr[8:, :] = _dot(xn, wxbc_ref[...])
    tail_scr[...] = xpad_scr[ts:ts + 8, :]
    conv = cb_ref[...] + cw_ref[CONV_W - 1:CONV_W, :] * xpad_scr[8:ts + 8, :]
    for k in range(CONV_W - 1):
        off = 8 - (CONV_W - 1) + k
        conv = conv + cw_ref[k:k + 1, :] * xpad_scr[off:off + ts, :]
    xc_scr[...] = _silu(conv)

    dt_scr[...] = jax.nn.softplus(_dot(xn, wdt_ref[...]) + dtb_ref[...])

    a_head = -jnp.exp(alog_ref[...])
    a_full = -jnp.exp(alogf_ref[...])
    row = lax.broadcasted_iota(jnp.int32, (CHUNK, 3 * CHUNK), 0)
    col = lax.broadcasted_iota(jnp.int32, (CHUNK, 3 * CHUNK), 1)
    tril3 = (col % CHUNK <= row).astype(BF16)
    lane_head = lax.broadcasted_iota(jnp.int32, (CHUNK, GROUP_W), 1) // HEAD_DIM

    def expand(v):
        hi, lo = _split2(v)
        return _dot(jnp.concatenate([hi, lo], axis=1), expand_ref[...])

    def chunk_body(c, carry):
        r0 = pl.multiple_of(c * CHUNK, CHUNK)
        dt_c = dt_scr[pl.ds(r0, CHUNK), :]
        a_c = dt_c * a_head
        hi = a_c.astype(BF16)
        r1 = a_c - hi.astype(F32)
        mid = r1.astype(BF16)
        lo = (r1 - mid.astype(F32)).astype(BF16)
        acs = _dot(tril3, jnp.concatenate([hi, mid, lo], axis=0))

        dt_f = expand(dt_c)
        acs_f = expand(acs)
        acs_t = jnp.sum(dt_f * a_full * triu_ref[...], axis=0, keepdims=True)
        lmat = jnp.exp(acs_f - acs_t + negmask_ref[...])
        acs_last = acs_f[CHUNK - 1:CHUNK, :]
        dec_out = jnp.exp(acs_f)
        dec_state = jnp.exp(acs_last - acs_f)
        dec_chunk = jnp.exp(acs_last)

        xs = xc_scr[pl.ds(r0, CHUNK), 0:D_INNER]
        xdt = xs * dt_f
        for g in range(SSM_GROUPS):
            gl = slice(g * GROUP_W, (g + 1) * GROUP_W)
            b_g = xc_scr[pl.ds(r0, CHUNK), D_INNER + g * D_STATE:D_INNER + (g + 1) * D_STATE]
            c_g = xc_scr[pl.ds(r0, CHUNK),
                         D_INNER + (SSM_GROUPS + g) * D_STATE:D_INNER + (SSM_GROUPS + g + 1) * D_STATE]
            b_bf = b_g.astype(BF16)
            c_bf = c_g.astype(BF16)
            cb4 = _dot_nt(c_bf, jnp.concatenate([b_bf] * 4, axis=0))
            w = (cb4 * lmat[:, gl]).astype(BF16)
            x_g = xdt[:, gl]
            bd = jnp.concatenate(
                [jnp.where(lane_head == r, x_g, 0.0) for r in range(4)], axis=0).astype(BF16)
            y_diag = _dot(w, bd)
            h_g = h_scr[:, gl]
            y_off = _dot(c_bf, h_g.astype(BF16)) * dec_out[:, gl]
            s_g = _dot(b_g.T.astype(BF16), (x_g * dec_state[:, gl]).astype(BF16))
            h_scr[:, gl] = h_g * dec_chunk[:, gl] + s_g
            y_scr[pl.ds(r0, CHUNK), gl] = y_diag + y_off + dskip_ref[:, gl] * xs[:, gl]
        return carry

    lax.fori_loop(0, ts // CHUNK, chunk_body, 0)

    gated = y_scr[...] * _silu(z_scr[...])
    parts = []
    for g in range(SSM_GROUPS):
        gg = gated[:, g * GROUP_W:(g + 1) * GROUP_W]
        parts.append(gg * lax.rsqrt(jnp.mean(gg * gg, axis=-1, keepdims=True) + EPS))
    yn = (jnp.concatenate(parts, axis=1) * onorm_ref[...]).astype(BF16)
    o_ref[0] = x + _dot(yn, wout_ref[...])


def _ssm_constants():
    h_of_lane = jnp.arange(D_INNER) // HEAD_DIM
    e0 = (jnp.arange(LANES)[:, None] == h_of_lane[None, :]).astype(BF16)
    expand = jnp.concatenate([e0, e0], axis=0)
    t_lane = jnp.arange(D_INNER) % HEAD_DIM
    t_row = jnp.arange(CHUNK)
    triu = (t_row[:, None] <= t_lane[None, :]).astype(F32)
    negmask = jnp.where(t_lane[None, :] <= t_row[:, None], 0.0, -jnp.inf).astype(F32)
    return expand, triu, negmask


def _ssm(x, g, w_in, conv_w, conv_b, dt_bias, a_log, d_skip, out_norm, w_out):
    b, s, _ = x.shape
    ts = min(SSM_ROWS, s)
    wz = w_in[:, :D_INNER].astype(BF16)
    wxbc = w_in[:, D_INNER:D_INNER + CONV_DIM].astype(BF16)
    wdt = jnp.pad(w_in[:, D_INNER + CONV_DIM:], ((0, 0), (0, LANES - SSM_HEADS))).astype(BF16)
    pad_h = lambda v: jnp.pad(v.reshape(1, SSM_HEADS), ((0, 0), (0, LANES - SSM_HEADS)))
    rep = lambda v: jnp.repeat(v, HEAD_DIM).reshape(1, D_INNER)
    expand, triu, negmask = _ssm_constants()
    operands = (
        x, g.reshape(1, D_MODEL), wz, wxbc, wdt, conv_w, conv_b.reshape(1, CONV_DIM),
        pad_h(dt_bias), pad_h(a_log), rep(a_log), rep(d_skip), out_norm.reshape(1, D_INNER),
        w_out.astype(BF16), expand, triu, negmask)
    in_specs = [pl.BlockSpec((1, ts, D_MODEL), lambda i, j: (i, j, 0))]
    in_specs += [_resident(op.shape) for op in operands[1:]]
    return pl.pallas_call(
        _ssm_kernel,
        grid=(b, s // ts),
        in_specs=in_specs,
        out_specs=pl.BlockSpec((1, ts, D_MODEL), lambda i, j: (i, j, 0)),
        out_shape=jax.ShapeDtypeStruct((b, s, D_MODEL), F32),
        scratch_shapes=[
            pltpu.VMEM((D_STATE, D_INNER), F32),
            pltpu.VMEM((8, CONV_DIM), F32),
            pltpu.VMEM((ts + 8, CONV_DIM), F32),
            pltpu.VMEM((ts, CONV_DIM), F32),
            pltpu.VMEM((ts, D_INNER), F32),
            pltpu.VMEM((ts, LANES), F32),
            pltpu.VMEM((ts, D_INNER), F32),
        ],
        compiler_params=_params("arbitrary", "arbitrary"),
        name="ssm",
    )(*operands)


def _head_norm_constants():
    c = jnp.arange(ATT_DIM) // HEAD_DIM
    s0 = (c[:, None] == jnp.arange(LANES)[None, :]).astype(BF16)
    return jnp.concatenate([s0, s0], axis=0), jnp.concatenate([s0.T, s0.T], axis=0)


def _head_norm(v, sum_ref, spread_ref):
    hi, lo = _split2(v * v)
    ms = _dot(jnp.concatenate([hi, lo], axis=1), sum_ref[...]) * (1.0 / HEAD_DIM)
    rh, rl = _split2(lax.rsqrt(ms + EPS))
    return v * _dot(jnp.concatenate([rh, rl], axis=1), spread_ref[...])


def _kv_kernel(x_ref, g_ref, wk_ref, wv_ref, kn_ref, sum_ref, spread_ref, k_ref, v_ref):
    i = pl.program_id(1)

    @pl.when(i == 0)
    def _():
        k_ref[...] = jnp.zeros_like(k_ref)
        v_ref[...] = jnp.zeros_like(v_ref)

    @pl.when(i > 0)
    def _():
        xn = _rms(x_ref[0], g_ref[...]).astype(BF16)
        k = _head_norm(_dot(xn, wk_ref[...]), sum_ref, spread_ref) * kn_ref[...]
        k_ref[0] = k.astype(BF16)
        v_ref[0] = _dot(xn, wv_ref[...]).astype(BF16)


def _shared_kv(x, g, w_kv, k_norm):
    b, s, _ = x.shape
    tk = PAD_LEN
    hsum, hspread = _head_norm_constants()
    out = jax.ShapeDtypeStruct((b, s + PAD_LEN, ATT_DIM), BF16)
    return pl.pallas_call(
        _kv_kernel,
        grid=(b, s // tk + 1),
        in_specs=[
            pl.BlockSpec((1, tk, D_MODEL), lambda i, j: (i, jnp.maximum(j - 1, 0), 0)),
            _resident((1, D_MODEL)),
            _resident((D_MODEL, ATT_DIM)),
            _resident((D_MODEL, ATT_DIM)),
            _resident((1, ATT_DIM)),
            _resident(hsum.shape),
            _resident(hspread.shape),
        ],
        out_specs=[pl.BlockSpec((1, tk, ATT_DIM), lambda i, j: (i, j, 0))] * 2,
        out_shape=[out, out],
        compiler_params=_params("arbitrary", "arbitrary"),
        name="shared_kv",
    )(x, g.reshape(1, D_MODEL), w_kv[:, :ATT_DIM].astype(BF16), w_kv[:, ATT_DIM:].astype(BF16),
      jnp.tile(k_norm, ATT_HEADS).reshape(1, ATT_DIM), hsum, hspread)


def _q_kernel(x_ref, g_ref, wq_ref, qn_ref, sum_ref, spread_ref, q_ref):
    xn = _rms(x_ref[...], g_ref[...]).astype(BF16)
    q = _head_norm(_dot(xn, wq_ref[...]), sum_ref, spread_ref) * qn_ref[...]
    q_ref[...] = (q * (1.0 / math.sqrt(HEAD_DIM))).astype(BF16)


def _q_proj(x2, g, w_q, q_norm):
    t = x2.shape[0]
    tm = min(PROJ_ROWS, t)
    hsum, hspread = _head_norm_constants()
    return pl.pallas_call(
        _q_kernel,
        grid=(t // tm,),
        in_specs=[
            pl.BlockSpec((tm, D_MODEL), lambda i: (i, 0)),
            _resident((1, D_MODEL)),
            _resident((D_MODEL, ATT_DIM)),
            _resident((1, ATT_DIM)),
            _resident(hsum.shape),
            _resident(hspread.shape),
        ],
        out_specs=pl.BlockSpec((tm, ATT_DIM), lambda i: (i, 0)),
        out_shape=jax.ShapeDtypeStruct((t, ATT_DIM), BF16),
        compiler_params=_params("arbitrary"),
        name="q_proj",
    )(x2, g.reshape(1, D_MODEL), w_q.astype(BF16), jnp.tile(q_norm, ATT_HEADS).reshape(1, ATT_DIM),
      hsum, hspread)


def _bias_kernel(u_ref, o_ref):
    rows = jnp.broadcast_to(u_ref[0], (CHUNK, TOEP_W))
    o_ref[0] = pltpu.roll(rows, 0, 1, stride=1, stride_axis=0)


def _rel_bias_table(rel_bias):
    far = rel_bias[:, :1]
    near = MAX_REL + CHUNK
    u = jnp.concatenate([
        jnp.broadcast_to(far, (ATT_HEADS, BAND - near)),
        rel_bias[:, :near],
        jnp.broadcast_to(far, (ATT_HEADS, TOEP_W - BAND)),
    ], axis=1).reshape(ATT_HEADS, 1, TOEP_W)
    return pl.pallas_call(
        _bias_kernel,
        grid=(ATT_HEADS,),
        in_specs=[pl.BlockSpec((1, 1, TOEP_W), lambda h: (h, 0, 0))],
        out_specs=pl.BlockSpec((1, CHUNK, TOEP_W), lambda h: (h, 0, 0)),
        out_shape=jax.ShapeDtypeStruct((ATT_HEADS, CHUNK, TOEP_W), F32),
        compiler_params=_params("arbitrary"),
        name="rel_bias",
    )(u)


def _attn_kernel(q_ref, k_ref, v_ref, bias_ref, o_ref):
    tq = q_ref.shape[1]
    n_chunks = tq // CHUNK
    qb = pl.program_id(2)
    base = pl.multiple_of(qb * tq, tq)
    win = tq + PAD_LEN
    lane = lax.broadcasted_iota(jnp.int32, (1, LANES), 1)
    head_mask = [(lane < HEAD_DIM).astype(BF16), (lane >= HEAD_DIM).astype(BF16)]
    k_win = k_ref[0, pl.ds(base, win), :]
    v_win = v_ref[0, pl.ds(base, win), :]
    v_heads = [v_win * m for m in head_mask]
    q = q_ref[0]
    key_pos = lax.broadcasted_iota(jnp.int32, (CHUNK, BAND), 1)
    for ci in range(n_chunks):
        r = slice(ci * CHUNK, (ci + 1) * CHUNK)
        band = slice(ci * CHUNK, ci * CHUNK + BAND)
        valid = key_pos >= PAD_LEN - (base + ci * CHUNK)
        acc = jnp.zeros((CHUNK, LANES), F32)
        for hh in range(2):
            s = _dot_nt(q[r] * head_mask[hh], k_win[band]) + bias_ref[hh, :, 0:BAND]
            s = jnp.where(valid, s, -jnp.inf)
            p = jnp.exp(s - jnp.max(s, axis=-1, keepdims=True))
            p = p / jnp.sum(p, axis=-1, keepdims=True)
            acc = acc + _dot(p.astype(BF16), v_heads[hh][band])
        o_ref[0, r, :] = acc.astype(BF16)


def _attention(q, k_pad, v_pad, bias):
    b, s, _ = q.shape
    tq = min(ATT_ROWS, s)
    return pl.pallas_call(
        _attn_kernel,
        grid=(b, ATT_HEADS // 2, s // tq),
        in_specs=[
            pl.BlockSpec((1, tq, LANES), lambda i, j, n: (i, n, j)),
            pl.BlockSpec((1, s + PAD_LEN, LANES), lambda i, j, n: (i, 0, j)),
            pl.BlockSpec((1, s + PAD_LEN, LANES), lambda i, j, n: (i, 0, j)),
            pl.BlockSpec((2, CHUNK, TOEP_W), lambda i, j, n: (j, 0, 0)),
        ],
        out_specs=pl.BlockSpec((1, tq, LANES), lambda i, j, n: (i, n, j)),
        out_shape=jax.ShapeDtypeStruct((b, s, ATT_DIM), BF16),
        compiler_params=_params("arbitrary", "arbitrary", "arbitrary"),
        name="chunk_attn",
    )(q, k_pad, v_pad, bias)


def _oproj_kernel(x_ref, o_ref, w_ref, out_ref):
    out_ref[...] = x_ref[...] + _dot(o_ref[...], w_ref[...])


def _out_proj(x2, o2, w_o):
    t = x2.shape[0]
    tm = min(PROJ_ROWS, t)
    return pl.pallas_call(
        _oproj_kernel,
        grid=(t // tm,),
        in_specs=[
            pl.BlockSpec((tm, D_MODEL), lambda i: (i, 0)),
            pl.BlockSpec((tm, ATT_DIM), lambda i: (i, 0)),
            _resident((ATT_DIM, D_MODEL)),
        ],
        out_specs=pl.BlockSpec((tm, D_MODEL), lambda i: (i, 0)),
        out_shape=jax.ShapeDtypeStruct((t, D_MODEL), F32),
        compiler_params=_params("arbitrary"),
        name="attn_out_proj",
    )(x2, o2, w_o.astype(BF16))


def kernel(x, ffn1_norm, ffn1_w_gate, ffn1_w_up, ffn1_w_down, ffn2_norm, ffn2_w_gate, ffn2_w_up,
           ffn2_w_down, ssm_norm, ssm_in_proj, ssm_conv_w, ssm_conv_b, ssm_dt_bias, ssm_A_log, ssm_D,
           ssm_out_norm, ssm_out_proj, kv_norm, w_kv, k_norm, att_norm, att_w_q, att_q_norm,
           att_rel_bias, att_w_o):
    b, s, d = x.shape
    depth = ffn1_norm.shape[0]
    n_ssm = ssm_norm.shape[0]
    t = b * s
    k_pad = v_pad = None
    for l in range(depth):
        x = _ffn(x.reshape(t, d), ffn1_norm[l], ffn1_w_gate[l].astype(BF16),
                 ffn1_w_up[l].astype(BF16), ffn1_w_down[l].astype(BF16)).reshape(b, s, d)
        if l < n_ssm:
            x = _ssm(x, ssm_norm[l], ssm_in_proj[l], ssm_conv_w[l], ssm_conv_b[l], ssm_dt_bias[l],
                     ssm_A_log[l], ssm_D[l], ssm_out_norm[l], ssm_out_proj[l])
        else:
            if l == n_ssm:
                k_pad, v_pad = _shared_kv(x, kv_norm, w_kv, k_norm)
            j = l - n_ssm
            q = _q_proj(x.reshape(t, d), att_norm[j], att_w_q[j], att_q_norm[j]).reshape(b, s, ATT_DIM)
            o = _attention(q, k_pad, v_pad, _rel_bias_table(att_rel_bias[j]))
            x = _out_proj(x.reshape(t, d), o.reshape(t, ATT_DIM), att_w_o[j]).reshape(b, s, d)
        x = _ffn(x.reshape(t, d), ffn2_norm[l], ffn2_w_gate[l].astype(BF16),
                 ffn2_w_up[l].astype(BF16), ffn2_w_down[l].astype(BF16)).reshape(b, s, d)
    return x
```

```python
import functools
import math

import jax
import jax.numpy as jnp
from jax import lax
from jax.experimental import pallas as pl
from jax.experimental.pallas import tpu as pltpu

F32 = jnp.float32
BF16 = jnp.bfloat16

D_MODEL = 1024
D_FF = 2816
EPS = 1e-6

D_INNER = 2048
HEAD_DIM = 64
SSM_HEADS = 32
SSM_GROUPS = 8
D_STATE = 128
CONV_W = 4
CONV_DIM = D_INNER + 2 * SSM_GROUPS * D_STATE
GROUP_W = D_INNER // SSM_GROUPS
CHUNK = 64

ATT_HEADS = 16
ATT_DIM = 1024
LEFT_CHUNKS = 8
BAND = (LEFT_CHUNKS + 1) * CHUNK
PAD_LEN = LEFT_CHUNKS * CHUNK
MAX_REL = 128

LANES = 128
VMEM_LIMIT = 56 * 1024 * 1024

FFN_ROWS = 512
SSM_ROWS = 256
ATT_ROWS = 256
ATT_WIN = ATT_ROWS + PAD_LEN
NEAR_ROWS = ATT_WIN - (PAD_LEN - MAX_REL)
FAR_MASK_CHUNKS = ATT_ROWS // CHUNK - 1
TOEP_W = 640


def _resident(shape):
    nd = len(shape)
    return pl.BlockSpec(shape, lambda *_: (0,) * nd, pipeline_mode=pl.Buffered(1))


def _params(*sem):
    return pltpu.CompilerParams(dimension_semantics=sem, vmem_limit_bytes=VMEM_LIMIT)


def _rms(x, g):
    return x * lax.rsqrt(jnp.mean(x * x, axis=-1, keepdims=True) + EPS) * g


def _dot(a, b):
    return jnp.dot(a, b, preferred_element_type=F32)


def _dot_nt(a, b):
    return lax.dot_general(a, b, (((1,), (1,)), ((), ())), preferred_element_type=F32)


def _dot_tn(a, b):
    return lax.dot_general(a, b, (((0,), (0,)), ((), ())), preferred_element_type=F32)


def _split2(v):
    hi = v.astype(BF16)
    lo = (v - hi.astype(F32)).astype(BF16)
    return hi, lo


def _silu(v):
    return v * jax.nn.sigmoid(v)


def _ffn_kernel(x_ref, g_ref, wg_ref, wu_ref, wd_ref, o_ref):
    x = x_ref[...]
    xn = _rms(x, g_ref[...]).astype(BF16)
    gate = _dot(xn, wg_ref[...])
    up = _dot(xn, wu_ref[...])
    h = (_silu(gate) * up).astype(BF16)
    o_ref[...] = x + 0.5 * _dot(h, wd_ref[...])


def _ffn(x2, g, wg, wu, wd):
    t = x2.shape[0]
    tm = min(FFN_ROWS, t)
    return pl.pallas_call(
        _ffn_kernel,
        grid=(t // tm,),
        in_specs=[
            pl.BlockSpec((tm, D_MODEL), lambda i: (i, 0)),
            _resident((1, D_MODEL)),
            _resident((D_MODEL, D_FF)),
            _resident((D_MODEL, D_FF)),
            _resident((D_FF, D_MODEL)),
        ],
        out_specs=pl.BlockSpec((tm, D_MODEL), lambda i: (i, 0)),
        out_shape=jax.ShapeDtypeStruct((t, D_MODEL), F32),
        compiler_params=_params("arbitrary"),
        name="ffn",
    )(x2, g.reshape(1, D_MODEL), wg, wu, wd)


def _ssm_kernel(x_ref, g_ref, wz_ref, wxbc_ref, wdt_ref, cw_ref, cb_ref, dtb_ref,
                alog_ref, alogf_ref, dskip_ref, onorm_ref, wout_ref,
                expand_ref, triu_ref, negmask_ref, o_ref,
                h_scr, tail_scr, xpad_scr, xc_scr, z_scr, dt_scr, y_scr):
    ts = x_ref.shape[1]

    @pl.when(pl.program_id(1) == 0)
    def _():
        h_scr[...] = jnp.zeros_like(h_scr)
        tail_scr[...] = jnp.zeros_like(tail_scr)

    x = x_ref[0]
    xn = _rms(x, g_ref[...]).astype(BF16)
    z_scr[...] = _dot(xn, wz_ref[...])

    xpad_scr[0:8, :] = tail_scr[...]
    xpad_scr[8:, :] = _dot(xn, wxbc_ref[...])
    tail_scr[...] = xpad_scr[ts:ts + 8, :]
    conv = cb_ref[...] + cw_ref[CONV_W - 1:CONV_W, :] * xpad_scr[8:ts + 8, :]
    for k in range(CONV_W - 1):
        off = 8 - (CONV_W - 1) + k
        conv = conv + cw_ref[k:k + 1, :] * xpad_scr[off:off + ts, :]
    xc_scr[...] = _silu(conv)

    dt_scr[...] = jax.nn.softplus(_dot(xn, wdt_ref[...]) + dtb_ref[...])

    a_head = -jnp.exp(alog_ref[...])
    a_full = -jnp.exp(alogf_ref[...])
    row = lax.broadcasted_iota(jnp.int32, (CHUNK, 3 * CHUNK), 0)
    col = lax.broadcasted_iota(jnp.int32, (CHUNK, 3 * CHUNK), 1)
    tril3 = (col % CHUNK <= row).astype(BF16)
    lane_head = lax.broadcasted_iota(jnp.int32, (CHUNK, GROUP_W), 1) // HEAD_DIM

    def expand(v):
        hi, lo = _split2(v)
        return _dot(jnp.concatenate([hi, lo], axis=1), expand_ref[...])

    def chunk_body(c, carry):
        r0 = pl.multiple_of(c * CHUNK, CHUNK)
        dt_c = dt_scr[pl.ds(r0, CHUNK), :]
        a_c = dt_c * a_head
        hi = a_c.astype(BF16)
        r1 = a_c - hi.astype(F32)
        mid = r1.astype(BF16)
        lo = (r1 - mid.astype(F32)).astype(BF16)
        acs = _dot(tril3, jnp.concatenate([hi, mid, lo], axis=0))

        dt_f = expand(dt_c)
        acs_f = expand(acs)
        acs_t = jnp.sum(dt_f * a_full * triu_ref[...], axis=0, keepdims=True)
        lmat = jnp.exp(acs_f - acs_t + negmask_ref[...])
        acs_last = acs_f[CHUNK - 1:CHUNK, :]
        dec_out = jnp.exp(acs_f)
        dec_state = jnp.exp(acs_last - acs_f)
        dec_chunk = jnp.exp(acs_last)

        xs = xc_scr[pl.ds(r0, CHUNK), 0:D_INNER]
        xdt = xs * dt_f
        for g in range(SSM_GROUPS):
            gl = slice(g * GROUP_W, (g + 1) * GROUP_W)
            b_g = xc_scr[pl.ds(r0, CHUNK), D_INNER + g * D_STATE:D_INNER + (g + 1) * D_STATE]
            c_g = xc_scr[pl.ds(r0, CHUNK),
                         D_INNER + (SSM_GROUPS + g) * D_STATE:D_INNER + (SSM_GROUPS + g + 1) * D_STATE]
            b_bf = b_g.astype(BF16)
            c_bf = c_g.astype(BF16)
            cb4 = _dot_nt(c_bf, jnp.concatenate([b_bf] * 4, axis=0))
            w = (cb4 * lmat[:, gl]).astype(BF16)
            x_g = xdt[:, gl]
            bd = jnp.concatenate(
                [jnp.where(lane_head == r, x_g, 0.0) for r in range(4)], axis=0).astype(BF16)
            y_diag = _dot(w, bd)
            h_g = h_scr[:, gl]
            y_off = _dot(c_bf, h_g.astype(BF16)) * dec_out[:, gl]
            s_g = _dot(b_g.T.astype(BF16), (x_g * dec_state[:, gl]).astype(BF16))
            h_scr[:, gl] = h_g * dec_chunk[:, gl] + s_g
            y_scr[pl.ds(r0, CHUNK), gl] = y_diag + y_off + dskip_ref[:, gl] * xs[:, gl]
        return carry

    lax.fori_loop(0, ts // CHUNK, chunk_body, 0)

    gated = y_scr[...] * _silu(z_scr[...])
    parts = []
    for g in range(SSM_GROUPS):
        gg = gated[:, g * GROUP_W:(g + 1) * GROUP_W]
        parts.append(gg * lax.rsqrt(jnp.mean(gg * gg, axis=-1, keepdims=True) + EPS))
    yn = (jnp.concatenate(parts, axis=1) * onorm_ref[...]).astype(BF16)
    o_ref[0] = x + _dot(yn, wout_ref[...])


def _ssm_constants():
    h_of_lane = jnp.arange(D_INNER) // HEAD_DIM
    e0 = (jnp.arange(LANES)[:, None] == h_of_lane[None, :]).astype(BF16)
    expand = jnp.concatenate([e0, e0], axis=0)
    t_lane = jnp.arange(D_INNER) % HEAD_DIM
    t_row = jnp.arange(CHUNK)
    triu = (t_row[:, None] <= t_lane[None, :]).astype(F32)
    negmask = jnp.where(t_lane[None, :] <= t_row[:, None], 0.0, -jnp.inf).astype(F32)
    return expand, triu, negmask


def _ssm(x, g, w_in, conv_w, conv_b, dt_bias, a_log, d_skip, out_norm, w_out):
    b, s, _ = x.shape
    ts = min(SSM_ROWS, s)
    wz = w_in[:, :D_INNER].astype(BF16)
    wxbc = w_in[:, D_INNER:D_INNER + CONV_DIM].astype(BF16)
    wdt = jnp.pad(w_in[:, D_INNER + CONV_DIM:], ((0, 0), (0, LANES - SSM_HEADS))).astype(BF16)
    pad_h = lambda v: jnp.pad(v.reshape(1, SSM_HEADS), ((0, 0), (0, LANES - SSM_HEADS)))
    rep = lambda v: jnp.repeat(v, HEAD_DIM).reshape(1, D_INNER)
    expand, triu, negmask = _ssm_constants()
    operands = (
        x, g.reshape(1, D_MODEL), wz, wxbc, wdt, conv_w, conv_b.reshape(1, CONV_DIM),
        pad_h(dt_bias), pad_h(a_log), rep(a_log), rep(d_skip), out_norm.reshape(1, D_INNER),
        w_out.astype(BF16), expand, triu, negmask)
    in_specs = [pl.BlockSpec((1, ts, D_MODEL), lambda i, j: (i, j, 0))]
    in_specs += [_resident(op.shape) for op in operands[1:]]
    return pl.pallas_call(
        _ssm_kernel,
        grid=(b, s // ts),
        in_specs=in_specs,
        out_specs=pl.BlockSpec((1, ts, D_MODEL), lambda i, j: (i, j, 0)),
        out_shape=jax.ShapeDtypeStruct((b, s, D_MODEL), F32),
        scratch_shapes=[
            pltpu.VMEM((D_STATE, D_INNER), F32),
            pltpu.VMEM((8, CONV_DIM), F32),
            pltpu.VMEM((ts + 8, CONV_DIM), F32),
            pltpu.VMEM((ts, CONV_DIM), F32),
            pltpu.VMEM((ts, D_INNER), F32),
            pltpu.VMEM((ts, LANES), F32),
            pltpu.VMEM((ts, D_INNER), F32),
        ],
        compiler_params=_params("arbitrary", "arbitrary"),
        name="ssm",
    )(*operands)


def _head_norm_constants():
    c = jnp.arange(ATT_DIM) // HEAD_DIM
    s0 = (c[:, None] == jnp.arange(LANES)[None, :]).astype(BF16)
    return jnp.concatenate([s0, s0], axis=0), jnp.concatenate([s0.T, s0.T], axis=0)


def _head_norm(v, sum_ref, spread_ref):
    hi, lo = _split2(v * v)
    ms = _dot(jnp.concatenate([hi, lo], axis=1), sum_ref[...]) * (1.0 / HEAD_DIM)
    rh, rl = _split2(lax.rsqrt(ms + EPS))
    return v * _dot(jnp.concatenate([rh, rl], axis=1), spread_ref[...])


def _kv_kernel(x_ref, g_ref, wk_ref, wvt_ref, kn_ref, sum_ref, spread_ref, k_ref, vt_ref):
    i = pl.program_id(1)

    @pl.when(i == 0)
    def _():
        k_ref[...] = jnp.zeros_like(k_ref)
        vt_ref[...] = jnp.zeros_like(vt_ref)

    @pl.when(i > 0)
    def _():
        xn = _rms(x_ref[0], g_ref[...]).astype(BF16)
        k = _head_norm(_dot(xn, wk_ref[...]), sum_ref, spread_ref) * kn_ref[...]
        k_ref[0] = k.astype(BF16)
        vt_ref[0] = _dot_nt(wvt_ref[...], xn).astype(BF16)


def _shared_kv(x, g, w_kv, k_norm):
    b, s, _ = x.shape
    tk = PAD_LEN
    hsum, hspread = _head_norm_constants()
    return pl.pallas_call(
        _kv_kernel,
        grid=(b, s // tk + 1),
        in_specs=[
            pl.BlockSpec((1, tk, D_MODEL), lambda i, j: (i, jnp.maximum(j - 1, 0), 0)),
            _resident((1, D_MODEL)),
            _resident((D_MODEL, ATT_DIM)),
            _resident((ATT_DIM, D_MODEL)),
            _resident((1, ATT_DIM)),
            _resident(hsum.shape),
            _resident(hspread.shape),
        ],
        out_specs=[pl.BlockSpec((1, tk, ATT_DIM), lambda i, j: (i, j, 0)),
                   pl.BlockSpec((1, ATT_DIM, tk), lambda i, j: (i, 0, j))],
        out_shape=[jax.ShapeDtypeStruct((b, s + PAD_LEN, ATT_DIM), BF16),
                   jax.ShapeDtypeStruct((b, ATT_DIM, s + PAD_LEN), BF16)],
        compiler_params=_params("arbitrary", "arbitrary"),
        name="shared_kv",
    )(x, g.reshape(1, D_MODEL), w_kv[:, :ATT_DIM].astype(BF16), w_kv[:, ATT_DIM:].T.astype(BF16),
      jnp.tile(k_norm, ATT_HEADS).reshape(1, ATT_DIM), hsum, hspread)


def _bias_kernel(w_ref, o_ref):
    rows = jnp.broadcast_to(w_ref[0], (NEAR_ROWS, TOEP_W))
    toep = pltpu.roll(rows, 0, 1, stride=1, stride_axis=0)[:, 0:ATT_ROWS]
    key_chunk = lax.broadcasted_iota(jnp.int32, (NEAR_ROWS, ATT_ROWS), 0) // CHUNK + (ATT_WIN - NEAR_ROWS) // CHUNK
    query_chunk = lax.broadcasted_iota(jnp.int32, (NEAR_ROWS, ATT_ROWS), 1) // CHUNK
    o_ref[0] = jnp.where(key_chunk - query_chunk <= LEFT_CHUNKS, toep, -jnp.inf)


def _rel_bias_tables(rel_bias):
    far = rel_bias[:, :1]
    w = jnp.concatenate([
        jnp.broadcast_to(far, (ATT_HEADS, ATT_ROWS + 1)),
        jnp.broadcast_to(rel_bias[:, -1:], (ATT_HEADS, TOEP_W - ATT_ROWS - 1 - 2 * MAX_REL)),
        rel_bias[:, :0:-1],
    ], axis=1).reshape(ATT_HEADS, 1, TOEP_W)
    near = pl.pallas_call(
        _bias_kernel,
        grid=(ATT_HEADS,),
        in_specs=[pl.BlockSpec((1, 1, TOEP_W), lambda h: (h, 0, 0))],
        out_specs=pl.BlockSpec((1, NEAR_ROWS, ATT_ROWS), lambda h: (h, 0, 0)),
        out_shape=jax.ShapeDtypeStruct((ATT_HEADS, NEAR_ROWS, ATT_ROWS), F32),
        compiler_params=_params("arbitrary"),
        name="rel_bias",
    )(w)
    return near, jnp.broadcast_to(far[:, :, None], (ATT_HEADS, 1, ATT_ROWS))


def _attn_layer_kernel(x_ref, g_ref, wq_ref, qn_ref, sum_ref, spread_ref, k_ref, vt_ref,
                       near_ref, far_ref, farmask_ref, wo_ref, o_ref, q_scr, ot_scr):
    base = pl.multiple_of(pl.program_id(1) * ATT_ROWS, ATT_ROWS)
    x = x_ref[0]
    xn = _rms(x, g_ref[...]).astype(BF16)
    q = _head_norm(_dot(xn, wq_ref[...]), sum_ref, spread_ref) * qn_ref[...]
    q_scr[...] = q.astype(BF16)

    lane = lax.broadcasted_iota(jnp.int32, (1, LANES), 1)
    head_mask = [(lane < HEAD_DIM).astype(BF16), (lane >= HEAD_DIM).astype(BF16)]
    n_kc = ATT_WIN // CHUNK
    far_kc = (ATT_WIN - NEAR_ROWS) // CHUNK
    pad_neg = [jnp.where(base + kc * CHUNK < PAD_LEN, -jnp.inf, 0.0).astype(F32)
               for kc in range(PAD_LEN // CHUNK)]

    def scores_t(h):
        l0 = (h // 2) * LANES
        q_pair = q_scr[:, pl.ds(l0, LANES)]
        k_win = k_ref[0, pl.ds(base, ATT_WIN), pl.ds(l0, LANES)]
        return _dot_nt(k_win, q_pair * head_mask[h % 2])

    st_next = scores_t(0)
    for h in range(ATT_HEADS):
        st = st_next
        if h + 1 < ATT_HEADS:
            st_next = scores_t(h + 1)
        far_row = far_ref[h]
        blocks = []
        for kc in range(n_kc):
            blk = st[kc * CHUNK:(kc + 1) * CHUNK, :]
            if kc < far_kc:
                blk = blk + (far_row + pad_neg[kc])
                if kc < FAR_MASK_CHUNKS:
                    blk = blk + farmask_ref[kc * CHUNK:(kc + 1) * CHUNK, :]
            else:
                blk = blk + near_ref[h, (kc - far_kc) * CHUNK:(kc - far_kc + 1) * CHUNK, :]
                if kc < len(pad_neg):
                    blk = blk + pad_neg[kc]
            blocks.append(blk)
        s = jnp.concatenate(blocks, axis=0)
        p = jnp.exp(s - jnp.max(s, axis=0, keepdims=True))
        denom = jnp.sum(p, axis=0, keepdims=True)
        r0 = h * HEAD_DIM
        vt = vt_ref[0, pl.ds(r0, HEAD_DIM), pl.ds(base, ATT_WIN)]
        ot = _dot(vt, p.astype(BF16)) / denom
        ot_scr[pl.ds(r0, HEAD_DIM), :] = ot.astype(BF16)
    o_ref[0] = x + _dot_tn(ot_scr[...], wo_ref[...])


def _attn_layer(x, k_pad, vt_pad, g, w_q, q_norm, rel_bias, w_o):
    b, s, _ = x.shape
    hsum, hspread = _head_norm_constants()
    near, far = _rel_bias_tables(rel_bias)
    key_chunk = jnp.arange(FAR_MASK_CHUNKS * CHUNK) // CHUNK
    query_chunk = jnp.arange(ATT_ROWS) // CHUNK
    farmask = jnp.where(key_chunk[:, None] >= query_chunk[None, :], 0.0, -jnp.inf).astype(F32)
    qn = jnp.tile(q_norm, ATT_HEADS).reshape(1, ATT_DIM) * (1.0 / math.sqrt(HEAD_DIM))
    return pl.pallas_call(
        _attn_layer_kernel,
        grid=(b, s // ATT_ROWS),
        in_specs=[
            pl.BlockSpec((1, ATT_ROWS, D_MODEL), lambda i, j: (i, j, 0)),
            _resident((1, D_MODEL)),
            _resident((D_MODEL, ATT_DIM)),
            _resident((1, ATT_DIM)),
            _resident(hsum.shape),
            _resident(hspread.shape),
            pl.BlockSpec((1, s + PAD_LEN, ATT_DIM), lambda i, j: (i, 0, 0), pipeline_mode=pl.Buffered(1)),
            pl.BlockSpec((1, ATT_DIM, s + PAD_LEN), lambda i, j: (i, 0, 0), pipeline_mode=pl.Buffered(1)),
            _resident(near.shape),
            _resident(far.shape),
            _resident(farmask.shape),
            _resident((ATT_DIM, D_MODEL)),
        ],
        out_specs=pl.BlockSpec((1, ATT_ROWS, D_MODEL), lambda i, j: (i, j, 0)),
        out_shape=jax.ShapeDtypeStruct((b, s, D_MODEL), F32),
        scratch_shapes=[
            pltpu.VMEM((ATT_ROWS, ATT_DIM), BF16),
            pltpu.VMEM((ATT_DIM, ATT_ROWS), BF16),
        ],
        compiler_params=_params("arbitrary", "arbitrary"),
        name="attn_layer",
    )(x, g.reshape(1, D_MODEL), w_q.astype(BF16), qn, hsum, hspread, k_pad, vt_pad,
      near, far, farmask, w_o.astype(BF16))


def kernel(x, ffn1_norm, ffn1_w_gate, ffn1_w_up, ffn1_w_down, ffn2_norm, ffn2_w_gate, ffn2_w_up,
           ffn2_w_down, ssm_norm, ssm_in_proj, ssm_conv_w, ssm_conv_b, ssm_dt_bias, ssm_A_log, ssm_D,
           ssm_out_norm, ssm_out_proj, kv_norm, w_kv, k_norm, att_norm, att_w_q, att_q_norm,
           att_rel_bias, att_w_o):
    b, s, d = x.shape
    depth = ffn1_norm.shape[0]
    n_ssm = ssm_norm.shape[0]
    t = b * s
    k_pad = vt_pad = None
    for l in range(depth):
        x = _ffn(x.reshape(t, d), ffn1_norm[l], ffn1_w_gate[l].astype(BF16),
                 ffn1_w_up[l].astype(BF16), ffn1_w_down[l].astype(BF16)).reshape(b, s, d)
        if l < n_ssm:
            x = _ssm(x, ssm_norm[l], ssm_in_proj[l], ssm_conv_w[l], ssm_conv_b[l], ssm_dt_bias[l],
                     ssm_A_log[l], ssm_D[l], ssm_out_norm[l], ssm_out_proj[l])
        else:
            if l == n_ssm:
                k_pad, vt_pad = _shared_kv(x, kv_norm, w_kv, k_norm)
            j = l - n_ssm
            x = _attn_layer(x, k_pad, vt_pad, att_norm[j], att_w_q[j], att_q_norm[j],
                            att_rel_bias[j], att_w_o[j])
        x = _ffn(x.reshape(t, d), ffn2_norm[l], ffn2_w_gate[l].astype(BF16),
                 ffn2_w_up[l].astype(BF16), ffn2_w_down[l].astype(BF16)).reshape(b, s, d)
    return x
```

```python
import functools
import math

import jax
import jax.numpy as jnp
from jax import lax
from jax.experimental import pallas as pl
from jax.experimental.pallas import tpu as pltpu

F32 = jnp.float32
BF16 = jnp.bfloat16

D_MODEL = 1024
D_FF = 2816
EPS = 1e-6

D_INNER = 2048
HEAD_DIM = 64
SSM_HEADS = 32
SSM_GROUPS = 8
D_STATE = 128
CONV_W = 4
CONV_DIM = D_INNER + 2 * SSM_GROUPS * D_STATE
GROUP_W = D_INNER // SSM_GROUPS
CHUNK = 64
CONV_COLS = 512

ATT_HEADS = 16
ATT_DIM = 1024
LEFT_CHUNKS = 8
BAND = (LEFT_CHUNKS + 1) * CHUNK
PAD_LEN = LEFT_CHUNKS * CHUNK
MAX_REL = 128

LANES = 128
VMEM_LIMIT = 56 * 1024 * 1024

FFN_ROWS = 512
SSM_ROWS = 256
Q_COLS = 256
SCORE_LOOKAHEAD = 2
ATT_ROWS = 256
ATT_WIN = ATT_ROWS + PAD_LEN
NEAR_ROWS = ATT_WIN - (PAD_LEN - MAX_REL)
FAR_MASK_CHUNKS = ATT_ROWS // CHUNK - 1
TOEP_W = 640


def _resident(shape):
    nd = len(shape)
    return pl.BlockSpec(shape, lambda *_: (0,) * nd, pipeline_mode=pl.Buffered(1))


def _params(*sem):
    return pltpu.CompilerParams(dimension_semantics=sem, vmem_limit_bytes=VMEM_LIMIT)


def _rms(x, g):
    return x * lax.rsqrt(jnp.mean(x * x, axis=-1, keepdims=True) + EPS) * g


def _dot(a, b):
    return jnp.dot(a, b, preferred_element_type=F32)


def _dot_nt(a, b):
    return lax.dot_general(a, b, (((1,), (1,)), ((), ())), preferred_element_type=F32)


def _dot_tn(a, b):
    return lax.dot_general(a, b, (((0,), (0,)), ((), ())), preferred_element_type=F32)


def _split2(v):
    hi = v.astype(BF16)
    lo = (v - hi.astype(F32)).astype(BF16)
    return hi, lo


def _silu(v):
    return v * jax.nn.sigmoid(v)


def _ffn_kernel(x_ref, g_ref, wg_ref, wu_ref, wd_ref, o_ref):
    x = x_ref[...]
    xn = _rms(x, g_ref[...]).astype(BF16)
    gate = _dot(xn, wg_ref[...])
    up = _dot(xn, wu_ref[...])
    h = (_silu(gate) * up).astype(BF16)
    o_ref[...] = x + 0.5 * _dot(h, wd_ref[...])


def _ffn(x2, g, wg, wu, wd):
    t = x2.shape[0]
    tm = min(FFN_ROWS, t)
    return pl.pallas_call(
        _ffn_kernel,
        grid=(t // tm,),
        in_specs=[
            pl.BlockSpec((tm, D_MODEL), lambda i: (i, 0)),
            _resident((1, D_MODEL)),
            _resident((D_MODEL, D_FF)),
            _resident((D_MODEL, D_FF)),
            _resident((D_FF, D_MODEL)),
        ],
        out_specs=pl.BlockSpec((tm, D_MODEL), lambda i: (i, 0)),
        out_shape=jax.ShapeDtypeStruct((t, D_MODEL), F32),
        compiler_params=_params("arbitrary"),
        name="ffn",
    )(x2, g.reshape(1, D_MODEL), wg, wu, wd)


def _ssm_kernel(x_ref, g_ref, wz_ref, wxbc_ref, wdt_ref, cw_ref, cb_ref, dtb_ref,
                alog_ref, alogf_ref, dskip_ref, onorm_ref, wout_ref,
                expand_ref, triu_ref, negmask_ref, o_ref,
                h_scr, tail_scr, xpad_scr, xs_scr, bc_scr, z_scr, dt_scr, y_scr):
    ts = x_ref.shape[1]

    @pl.when(pl.program_id(1) == 0)
    def _():
        h_scr[...] = jnp.zeros_like(h_scr)
        tail_scr[...] = jnp.zeros_like(tail_scr)

    x = x_ref[0]
    xn = _rms(x, g_ref[...]).astype(BF16)

    n_grp = CONV_DIM // CONV_COLS
    n_xs = D_INNER // CONV_COLS

    def project(g):
        return _dot(xn, wxbc_ref[:, g * CONV_COLS:(g + 1) * CONV_COLS])

    def conv_group(g, pre):
        cols = slice(g * CONV_COLS, (g + 1) * CONV_COLS)
        slot = g % 2
        xpad_scr[slot, 0:8, :] = tail_scr[:, cols]
        xpad_scr[slot, 8:, :] = pre
        tail_scr[:, cols] = pre[ts - 8:, :]
        conv = cb_ref[:, cols] + cw_ref[CONV_W - 1:CONV_W, cols] * pre
        for k in range(CONV_W - 1):
            off = 8 - (CONV_W - 1) + k
            conv = conv + cw_ref[k:k + 1, cols] * xpad_scr[slot, off:off + ts, :]
        act = _silu(conv)
        if g < n_xs:
            xs_scr[:, cols] = act
        else:
            bc_scr[:, (g - n_xs) * CONV_COLS:(g - n_xs + 1) * CONV_COLS] = act.astype(BF16)

    pre_next = project(0)
    for g in range(n_grp):
        pre = pre_next
        if g + 1 < n_grp:
            pre_next = project(g + 1)
        if g % 2 == 1:
            zc = slice((g // 2) * CONV_COLS, (g // 2 + 1) * CONV_COLS)
            z_scr[:, zc] = _dot(xn, wz_ref[:, zc])
        conv_group(g, pre)

    dt_scr[...] = jax.nn.softplus(_dot(xn, wdt_ref[...]) + dtb_ref[...])

    a_head = -jnp.exp(alog_ref[...])
    a_full = -jnp.exp(alogf_ref[...])
    row = lax.broadcasted_iota(jnp.int32, (CHUNK, 3 * CHUNK), 0)
    col = lax.broadcasted_iota(jnp.int32, (CHUNK, 3 * CHUNK), 1)
    tril3 = (col % CHUNK <= row).astype(BF16)
    lane_head = lax.broadcasted_iota(jnp.int32, (1, GROUP_W), 1) // HEAD_DIM
    head_lanes = [(lane_head == r).astype(BF16) for r in range(GROUP_W // HEAD_DIM)]
    groups = range(SSM_GROUPS)
    gls = [slice(g * GROUP_W, (g + 1) * GROUP_W) for g in groups]

    def expand(v):
        hi, lo = _split2(v)
        return _dot(jnp.concatenate([hi, lo], axis=1), expand_ref[...])

    def chunk_body(c, carry):
        r0 = pl.multiple_of(c * CHUNK, CHUNK)
        rows = pl.ds(r0, CHUNK)
        b_bf = [bc_scr[rows, g * D_STATE:(g + 1) * D_STATE] for g in groups]
        c_bf = [bc_scr[rows, (SSM_GROUPS + g) * D_STATE:(SSM_GROUPS + g + 1) * D_STATE] for g in groups]
        cb4 = [_dot_nt(c_bf[g], jnp.concatenate([b_bf[g]] * 4, axis=0)) for g in groups]

        dt_c = dt_scr[rows, :]
        a_c = dt_c * a_head
        hi = a_c.astype(BF16)
        r1 = a_c - hi.astype(F32)
        mid = r1.astype(BF16)
        lo = (r1 - mid.astype(F32)).astype(BF16)
        acs = _dot(tril3, jnp.concatenate([hi, mid, lo], axis=0))

        dt_f = expand(dt_c)
        acs_f = expand(acs)
        y_off = [_dot(c_bf[g], h_scr[:, gls[g]].astype(BF16)) for g in groups]
        acs_t = jnp.sum(dt_f * a_full * triu_ref[...], axis=0, keepdims=True)
        acs_last = acs_f[CHUNK - 1:CHUNK, :]
        dec_chunk = jnp.exp(acs_last)
        xs = xs_scr[rows, :]
        xdt = xs * dt_f
        xdec = (xdt * jnp.exp(acs_last - acs_f)).astype(BF16)
        for g in groups:
            s_g = _dot_tn(b_bf[g], xdec[:, gls[g]])
            h_scr[:, gls[g]] = h_scr[:, gls[g]] * dec_chunk[:, gls[g]] + s_g

        lmat = jnp.exp(acs_f - acs_t + negmask_ref[...])
        dec_out = jnp.exp(acs_f)
        xdt_bf = xdt.astype(BF16)
        for g in groups:
            w = (cb4[g] * lmat[:, gls[g]]).astype(BF16)
            x_g = xdt_bf[:, gls[g]]
            bd = jnp.concatenate([x_g * m for m in head_lanes], axis=0)
            y_scr[rows, gls[g]] = (_dot(w, bd) + y_off[g] * dec_out[:, gls[g]]
                                   + dskip_ref[:, gls[g]] * xs[:, gls[g]])
        return carry

    lax.fori_loop(0, ts // CHUNK, chunk_body, 0, unroll=True)

    out = x
    half = D_INNER // 2
    for hf in range(2):
        parts = []
        for g in range(hf * SSM_GROUPS // 2, (hf + 1) * SSM_GROUPS // 2):
            gg = y_scr[:, gls[g]] * _silu(z_scr[:, gls[g]])
            parts.append(gg * lax.rsqrt(jnp.mean(gg * gg, axis=-1, keepdims=True) + EPS))
        cols = slice(hf * half, (hf + 1) * half)
        yn = (jnp.concatenate(parts, axis=1) * onorm_ref[:, cols]).astype(BF16)
        out = out + _dot(yn, wout_ref[cols, :])
    o_ref[0] = out


def _ssm_constants():
    h_of_lane = jnp.arange(D_INNER) // HEAD_DIM
    e0 = (jnp.arange(LANES)[:, None] == h_of_lane[None, :]).astype(BF16)
    expand = jnp.concatenate([e0, e0], axis=0)
    t_lane = jnp.arange(D_INNER) % HEAD_DIM
    t_row = jnp.arange(CHUNK)
    triu = (t_row[:, None] <= t_lane[None, :]).astype(F32)
    negmask = jnp.where(t_lane[None, :] <= t_row[:, None], 0.0, -jnp.inf).astype(F32)
    return expand, triu, negmask


def _ssm(x, g, w_in, conv_w, conv_b, dt_bias, a_log, d_skip, out_norm, w_out):
    b, s, _ = x.shape
    ts = min(SSM_ROWS, s)
    wz = w_in[:, :D_INNER].astype(BF16)
    wxbc = w_in[:, D_INNER:D_INNER + CONV_DIM].astype(BF16)
    wdt = jnp.pad(w_in[:, D_INNER + CONV_DIM:], ((0, 0), (0, LANES - SSM_HEADS))).astype(BF16)
    pad_h = lambda v: jnp.pad(v.reshape(1, SSM_HEADS), ((0, 0), (0, LANES - SSM_HEADS)))
    rep = lambda v: jnp.repeat(v, HEAD_DIM).reshape(1, D_INNER)
    expand, triu, negmask = _ssm_constants()
    operands = (
        x, g.reshape(1, D_MODEL), wz, wxbc, wdt, conv_w, conv_b.reshape(1, CONV_DIM),
        pad_h(dt_bias), pad_h(a_log), rep(a_log), rep(d_skip), out_norm.reshape(1, D_INNER),
        w_out.astype(BF16), expand, triu, negmask)
    in_specs = [pl.BlockSpec((1, ts, D_MODEL), lambda i, j: (i, j, 0))]
    in_specs += [_resident(op.shape) for op in operands[1:]]
    return pl.pallas_call(
        _ssm_kernel,
        grid=(b, s // ts),
        in_specs=in_specs,
        out_specs=pl.BlockSpec((1, ts, D_MODEL), lambda i, j: (i, j, 0)),
        out_shape=jax.ShapeDtypeStruct((b, s, D_MODEL), F32),
        scratch_shapes=[
            pltpu.VMEM((D_STATE, D_INNER), F32),
            pltpu.VMEM((8, CONV_DIM), F32),
            pltpu.VMEM((2, ts + 8, CONV_COLS), F32),
            pltpu.VMEM((ts, D_INNER), F32),
            pltpu.VMEM((ts, CONV_DIM - D_INNER), BF16),
            pltpu.VMEM((ts, D_INNER), F32),
            pltpu.VMEM((ts, LANES), F32),
            pltpu.VMEM((ts, D_INNER), F32),
        ],
        compiler_params=_params("arbitrary", "arbitrary"),
        name="ssm",
    )(*operands)


def _head_norm_constants(width=ATT_DIM):
    c = jnp.arange(width) // HEAD_DIM
    s0 = (c[:, None] == jnp.arange(LANES)[None, :]).astype(BF16)
    return jnp.concatenate([s0, s0], axis=0), jnp.concatenate([s0.T, s0.T], axis=0)


def _head_norm(v, sum_ref, spread_ref):
    hi, lo = _split2(v * v)
    ms = _dot(jnp.concatenate([hi, lo], axis=1), sum_ref[...]) * (1.0 / HEAD_DIM)
    rh, rl = _split2(lax.rsqrt(ms + EPS))
    return v * _dot(jnp.concatenate([rh, rl], axis=1), spread_ref[...])


def _kv_kernel(x_ref, g_ref, wk_ref, wvt_ref, kn_ref, sum_ref, spread_ref, k_ref, vt_ref):
    i = pl.program_id(1)

    @pl.when(i == 0)
    def _():
        k_ref[...] = jnp.zeros_like(k_ref)
        vt_ref[...] = jnp.zeros_like(vt_ref)

    @pl.when(i > 0)
    def _():
        xn = _rms(x_ref[0], g_ref[...]).astype(BF16)
        k = _head_norm(_dot(xn, wk_ref[...]), sum_ref, spread_ref) * kn_ref[...]
        k_ref[0] = k.astype(BF16)
        vt_ref[0] = _dot_nt(wvt_ref[...], xn).astype(BF16)


def _shared_kv(x, g, w_kv, k_norm):
    b, s, _ = x.shape
    tk = PAD_LEN
    hsum, hspread = _head_norm_constants()
    return pl.pallas_call(
        _kv_kernel,
        grid=(b, s // tk + 1),
        in_specs=[
            pl.BlockSpec((1, tk, D_MODEL), lambda i, j: (i, jnp.maximum(j - 1, 0), 0)),
            _resident((1, D_MODEL)),
            _resident((D_MODEL, ATT_DIM)),
            _resident((ATT_DIM, D_MODEL)),
            _resident((1, ATT_DIM)),
            _resident(hsum.shape),
            _resident(hspread.shape),
        ],
        out_specs=[pl.BlockSpec((1, tk, ATT_DIM), lambda i, j: (i, j, 0)),
                   pl.BlockSpec((1, ATT_DIM, tk), lambda i, j: (i, 0, j))],
        out_shape=[jax.ShapeDtypeStruct((b, s + PAD_LEN, ATT_DIM), BF16),
                   jax.ShapeDtypeStruct((b, ATT_DIM, s + PAD_LEN), BF16)],
        compiler_params=_params("arbitrary", "arbitrary"),
        name="shared_kv",
    )(x, g.reshape(1, D_MODEL), w_kv[:, :ATT_DIM].astype(BF16), w_kv[:, ATT_DIM:].T.astype(BF16),
      jnp.tile(k_norm, ATT_HEADS).reshape(1, ATT_DIM), hsum, hspread)


def _bias_kernel(w_ref, o_ref):
    rows = jnp.broadcast_to(w_ref[0], (NEAR_ROWS, TOEP_W))
    toep = pltpu.roll(rows, 0, 1, stride=1, stride_axis=0)[:, 0:ATT_ROWS]
    key_chunk = lax.broadcasted_iota(jnp.int32, (NEAR_ROWS, ATT_ROWS), 0) // CHUNK + (ATT_WIN - NEAR_ROWS) // CHUNK
    query_chunk = lax.broadcasted_iota(jnp.int32, (NEAR_ROWS, ATT_ROWS), 1) // CHUNK
    o_ref[0] = jnp.where(key_chunk - query_chunk <= LEFT_CHUNKS, toep, -jnp.inf)


def _rel_bias_tables(rel_bias):
    far = rel_bias[:, :1]
    w = jnp.concatenate([
        jnp.broadcast_to(far, (ATT_HEADS, ATT_ROWS + 1)),
        jnp.broadcast_to(rel_bias[:, -1:], (ATT_HEADS, TOEP_W - ATT_ROWS - 1 - 2 * MAX_REL)),
        rel_bias[:, :0:-1],
    ], axis=1).reshape(ATT_HEADS, 1, TOEP_W)
    near = pl.pallas_call(
        _bias_kernel,
        grid=(ATT_HEADS,),
        in_specs=[pl.BlockSpec((1, 1, TOEP_W), lambda h: (h, 0, 0))],
        out_specs=pl.BlockSpec((1, NEAR_ROWS, ATT_ROWS), lambda h: (h, 0, 0)),
        out_shape=jax.ShapeDtypeStruct((ATT_HEADS, NEAR_ROWS, ATT_ROWS), F32),
        compiler_params=_params("arbitrary"),
        name="rel_bias",
    )(w)
    return near, jnp.broadcast_to(far[:, :, None], (ATT_HEADS, 1, ATT_ROWS))


def _attn_layer_kernel(x_ref, g_ref, wq_ref, qn_ref, sum_ref, spread_ref, k_ref, vt_ref,
                       near_ref, far_ref, farmask_ref, wo_ref, o_ref, q_scr, ot_scr):
    base = pl.multiple_of(pl.program_id(1) * ATT_ROWS, ATT_ROWS)
    x = x_ref[0]
    xn = _rms(x, g_ref[...]).astype(BF16)

    def q_project(g):
        return _dot(xn, wq_ref[:, g * Q_COLS:(g + 1) * Q_COLS])

    def q_mean_sq(qraw):
        hi, lo = _split2(qraw * qraw)
        return _dot(jnp.concatenate([hi, lo], axis=1), sum_ref[...]) * (1.0 / HEAD_DIM)

    def q_inv_rms(ms):
        rh, rl = _split2(lax.rsqrt(ms + EPS))
        return _dot(jnp.concatenate([rh, rl], axis=1), spread_ref[...])

    def q_store(g, qraw, inv):
        cols = slice(g * Q_COLS, (g + 1) * Q_COLS)
        q_scr[:, cols] = (qraw * inv * qn_ref[:, cols]).astype(BF16)

    qraw = q_project(0)
    q_store(0, qraw, q_inv_rms(q_mean_sq(qraw)))

    lane = lax.broadcasted_iota(jnp.int32, (1, LANES), 1)
    head_mask = [(lane < HEAD_DIM).astype(BF16), (lane >= HEAD_DIM).astype(BF16)]
    n_kc = ATT_WIN // CHUNK
    far_kc = (ATT_WIN - NEAR_ROWS) // CHUNK
    pad_neg = [jnp.where(base + kc * CHUNK < PAD_LEN, -jnp.inf, 0.0).astype(F32)
               for kc in range(PAD_LEN // CHUNK)]

    def scores_t(h):
        l0 = (h // 2) * LANES
        q_pair = q_scr[:, pl.ds(l0, LANES)]
        k_win = k_ref[0, pl.ds(base, ATT_WIN), pl.ds(l0, LANES)]
        return _dot_nt(k_win, q_pair * head_mask[h % 2])

    heads_per_group = Q_COLS // HEAD_DIM
    half_dim = ATT_DIM // 2
    out = x
    qraw = q_project(1)
    pending = [scores_t(h) for h in range(SCORE_LOOKAHEAD)]
    for h in range(ATT_HEADS):
        st = pending.pop(0)
        g_next, stage = divmod(h + SCORE_LOOKAHEAD + heads_per_group - 1, heads_per_group)
        if 1 <= g_next < ATT_DIM // Q_COLS:
            if stage == 0:
                qraw = q_project(g_next)
            elif stage == 1:
                ms = q_mean_sq(qraw)
            elif stage == 2:
                inv = q_inv_rms(ms)
            else:
                q_store(g_next, qraw, inv)
        if h + SCORE_LOOKAHEAD < ATT_HEADS:
            pending.append(scores_t(h + SCORE_LOOKAHEAD))
        far_row = far_ref[h]
        blocks = []
        for kc in range(n_kc):
            blk = st[kc * CHUNK:(kc + 1) * CHUNK, :]
            if kc < far_kc:
                blk = blk + (far_row + pad_neg[kc])
                if kc < FAR_MASK_CHUNKS:
                    blk = blk + farmask_ref[kc * CHUNK:(kc + 1) * CHUNK, :]
            else:
                blk = blk + near_ref[h, (kc - far_kc) * CHUNK:(kc - far_kc + 1) * CHUNK, :]
                if kc < len(pad_neg):
                    blk = blk + pad_neg[kc]
            blocks.append(blk)
        s = jnp.concatenate(blocks, axis=0)
        p = jnp.exp2(s - jnp.max(s, axis=0, keepdims=True))
        denom = jnp.sum(p, axis=0, keepdims=True)
        r0 = h * HEAD_DIM
        vt = vt_ref[0, pl.ds(r0, HEAD_DIM), pl.ds(base, ATT_WIN)]
        ot = _dot(vt, p.astype(BF16)) / denom
        ot_scr[pl.ds(r0, HEAD_DIM), :] = ot.astype(BF16)
        if h == ATT_HEADS // 2:
            out = out + _dot_tn(ot_scr[0:half_dim, :], wo_ref[0:half_dim, :])
    o_ref[0] = out + _dot_tn(ot_scr[half_dim:, :], wo_ref[half_dim:, :])


def _attn_layer(x, k_pad, vt_pad, g, w_q, q_norm, rel_bias, w_o):
    b, s, _ = x.shape
    hsum, hspread = _head_norm_constants(Q_COLS)
    log2e = 1.0 / math.log(2.0)
    near, far = _rel_bias_tables(rel_bias * log2e)
    key_chunk = jnp.arange(FAR_MASK_CHUNKS * CHUNK) // CHUNK
    query_chunk = jnp.arange(ATT_ROWS) // CHUNK
    farmask = jnp.where(key_chunk[:, None] >= query_chunk[None, :], 0.0, -jnp.inf).astype(F32)
    qn = jnp.tile(q_norm, ATT_HEADS).reshape(1, ATT_DIM) * (log2e / math.sqrt(HEAD_DIM))
    return pl.pallas_call(
        _attn_layer_kernel,
        grid=(b, s // ATT_ROWS),
        in_specs=[
            pl.BlockSpec((1, ATT_ROWS, D_MODEL), lambda i, j: (i, j, 0)),
            _resident((1, D_MODEL)),
            _resident((D_MODEL, ATT_DIM)),
            _resident((1, ATT_DIM)),
            _resident(hsum.shape),
            _resident(hspread.shape),
            pl.BlockSpec((1, s + PAD_LEN, ATT_DIM), lambda i, j: (i, 0, 0), pipeline_mode=pl.Buffered(1)),
            pl.BlockSpec((1, ATT_DIM, s + PAD_LEN), lambda i, j: (i, 0, 0), pipeline_mode=pl.Buffered(1)),
            _resident(near.shape),
            _resident(far.shape),
            _resident(farmask.shape),
            _resident((ATT_DIM, D_MODEL)),
        ],
        out_specs=pl.BlockSpec((1, ATT_ROWS, D_MODEL), lambda i, j: (i, j, 0)),
        out_shape=jax.ShapeDtypeStruct((b, s, D_MODEL), F32),
        scratch_shapes=[
            pltpu.VMEM((ATT_ROWS, ATT_DIM), BF16),
            pltpu.VMEM((ATT_DIM, ATT_ROWS), BF16),
        ],
        compiler_params=_params("arbitrary", "arbitrary"),
        name="attn_layer",
    )(x, g.reshape(1, D_MODEL), w_q.astype(BF16), qn, hsum, hspread, k_pad, vt_pad,
      near, far, farmask, w_o.astype(BF16))


def kernel(x, ffn1_norm, ffn1_w_gate, ffn1_w_up, ffn1_w_down, ffn2_norm, ffn2_w_gate, ffn2_w_up,
           ffn2_w_down, ssm_norm, ssm_in_proj, ssm_conv_w, ssm_conv_b, ssm_dt_bias, ssm_A_log, ssm_D,
           ssm_out_norm, ssm_out_proj, kv_norm, w_kv, k_norm, att_norm, att_w_q, att_q_norm,
           att_rel_bias, att_w_o):
    b, s, d = x.shape
    depth = ffn1_norm.shape[0]
    n_ssm = ssm_norm.shape[0]
    t = b * s
    k_pad = vt_pad = None
    for l in range(depth):
        x = _ffn(x.reshape(t, d), ffn1_norm[l], ffn1_w_gate[l].astype(BF16),
                 ffn1_w_up[l].astype(BF16), ffn1_w_down[l].astype(BF16)).reshape(b, s, d)
        if l < n_ssm:
            x = _ssm(x, ssm_norm[l], ssm_in_proj[l], ssm_conv_w[l], ssm_conv_b[l], ssm_dt_bias[l],
                     ssm_A_log[l], ssm_D[l], ssm_out_norm[l], ssm_out_proj[l])
        else:
            if l == n_ssm:
                k_pad, vt_pad = _shared_kv(x, kv_norm, w_kv, k_norm)
            j = l - n_ssm
            x = _attn_layer(x, k_pad, vt_pad, att_norm[j], att_w_q[j], att_q_norm[j],
                            att_rel_bias[j], att_w_o[j])
        x = _ffn(x.reshape(t, d), ffn2_norm[l], ffn2_w_gate[l].astype(BF16),
                 ffn2_w_up[l].astype(BF16), ffn2_w_down[l].astype(BF16)).reshape(b, s, d)
    return x
```

```python
import functools
import math

import jax
import jax.numpy as jnp
from jax import lax
from jax.experimental import pallas as pl
from jax.experimental.pallas import tpu as pltpu

F32 = jnp.float32
BF16 = jnp.bfloat16

D_MODEL = 1024
D_FF = 2816
EPS = 1e-6

D_INNER = 2048
HEAD_DIM = 64
SSM_HEADS = 32
SSM_GROUPS = 8
D_STATE = 128
CONV_W = 4
CONV_DIM = D_INNER + 2 * SSM_GROUPS * D_STATE
GROUP_W = D_INNER // SSM_GROUPS
CHUNK = 64
CONV_COLS = 512

ATT_HEADS = 16
ATT_DIM = 1024
LEFT_CHUNKS = 8
BAND = (LEFT_CHUNKS + 1) * CHUNK
PAD_LEN = LEFT_CHUNKS * CHUNK
MAX_REL = 128

LANES = 128
VMEM_LIMIT = 56 * 1024 * 1024

FFN_ROWS = 512
SSM_ROWS = 256
Q_COLS = 256
SCORE_LOOKAHEAD = 2
ATT_ROWS = 256
ATT_WIN = ATT_ROWS + PAD_LEN
NEAR_ROWS = ATT_WIN - (PAD_LEN - MAX_REL)
FAR_MASK_CHUNKS = ATT_ROWS // CHUNK - 1
TOEP_W = 640


def _resident(shape):
    nd = len(shape)
    return pl.BlockSpec(shape, lambda *_: (0,) * nd, pipeline_mode=pl.Buffered(1))


def _params(*sem):
    return pltpu.CompilerParams(dimension_semantics=sem, vmem_limit_bytes=VMEM_LIMIT)


def _rms(x, g):
    return x * lax.rsqrt(jnp.mean(x * x, axis=-1, keepdims=True) + EPS) * g


def _dot(a, b):
    return jnp.dot(a, b, preferred_element_type=F32)


def _dot_nt(a, b):
    return lax.dot_general(a, b, (((1,), (1,)), ((), ())), preferred_element_type=F32)


def _dot_tn(a, b):
    return lax.dot_general(a, b, (((0,), (0,)), ((), ())), preferred_element_type=F32)


def _split2(v):
    hi = v.astype(BF16)
    lo = (v - hi.astype(F32)).astype(BF16)
    return hi, lo


def _silu(v):
    return v * jax.nn.sigmoid(v)


def _ffn_kernel(x_ref, g_ref, wg_ref, wu_ref, wd_ref, o_ref):
    half = x_ref.shape[0] // 2
    hidden = []
    for r in range(2):
        x = x_ref[r * half:(r + 1) * half, :]
        xn = _rms(x, g_ref[...]).astype(BF16)
        hidden.append((_dot(xn, wg_ref[...]), _dot(xn, wu_ref[...])))
    for r in range(2):
        gate, up = hidden[r]
        h = (_silu(gate) * up).astype(BF16)
        rows = slice(r * half, (r + 1) * half)
        o_ref[rows, :] = x_ref[rows, :] + 0.5 * _dot(h, wd_ref[...])


def _ffn(x2, g, wg, wu, wd):
    t = x2.shape[0]
    tm = min(FFN_ROWS, t)
    return pl.pallas_call(
        _ffn_kernel,
        grid=(t // tm,),
        in_specs=[
            pl.BlockSpec((tm, D_MODEL), lambda i: (i, 0)),
            _resident((1, D_MODEL)),
            _resident((D_MODEL, D_FF)),
            _resident((D_MODEL, D_FF)),
            _resident((D_FF, D_MODEL)),
        ],
        out_specs=pl.BlockSpec((tm, D_MODEL), lambda i: (i, 0)),
        out_shape=jax.ShapeDtypeStruct((t, D_MODEL), F32),
        compiler_params=_params("arbitrary"),
        name="ffn",
    )(x2, g.reshape(1, D_MODEL), wg, wu, wd)


def _ssm_kernel(x_ref, g_ref, win_ref, wdt_ref, cw_ref, cb_ref, dtb_ref,
                alog_ref, alogf_ref, dskip_ref, onorm_ref, wout_ref,
                expand_ref, triu_ref, negmask_ref, tril_ref, o_ref,
                h_scr, tail_scr, xpad_scr, xs_scr, bc_scr, z_scr, dtf_scr, acsf_scr, y_scr):
    ts = x_ref.shape[1]

    @pl.when(pl.program_id(1) == 0)
    def _():
        h_scr[...] = jnp.zeros_like(h_scr)
        tail_scr[...] = jnp.zeros_like(tail_scr)

    x = x_ref[0]
    xn = _rms(x, g_ref[...]).astype(BF16)

    n_grp = CONV_DIM // CONV_COLS
    n_xs = D_INNER // CONV_COLS

    def project(g):
        return _dot(xn, win_ref[:, D_INNER + g * CONV_COLS:D_INNER + (g + 1) * CONV_COLS])

    def conv_group(g, pre):
        cols = slice(g * CONV_COLS, (g + 1) * CONV_COLS)
        slot = g % 2
        xpad_scr[slot, 0:8, :] = tail_scr[:, cols]
        xpad_scr[slot, 8:, :] = pre
        tail_scr[:, cols] = pre[ts - 8:, :]
        conv = cb_ref[:, cols] + cw_ref[CONV_W - 1:CONV_W, cols] * pre
        for k in range(CONV_W - 1):
            off = 8 - (CONV_W - 1) + k
            conv = conv + cw_ref[k:k + 1, cols] * xpad_scr[slot, off:off + ts, :]
        act = _silu(conv)
        if g < n_xs:
            xs_scr[:, cols] = act
        else:
            bc_scr[:, (g - n_xs) * CONV_COLS:(g - n_xs + 1) * CONV_COLS] = act.astype(BF16)

    a_head = -jnp.exp(alog_ref[...])
    a_full = -jnp.exp(alogf_ref[...])

    def expand(v):
        hi, lo = _split2(v)
        return _dot(jnp.concatenate([hi, lo], axis=1), expand_ref[...])

    pre_next = project(0)
    for g in range(n_grp):
        pre = pre_next
        if g + 1 < n_grp:
            pre_next = project(g + 1)
        if g % 2 == 1:
            zc = slice((g // 2) * CONV_COLS, (g // 2 + 1) * CONV_COLS)
            z_scr[:, zc] = _dot(xn, win_ref[:, zc])
        if g == 0:
            dt = jax.nn.softplus(_dot(xn, wdt_ref[...]) + dtb_ref[...])
        elif g == 2:
            dtf_scr[...] = expand(dt)
        elif g == 4:
            a = dt * a_head
            hi = a.astype(BF16)
            r1 = a - hi.astype(F32)
            mid = r1.astype(BF16)
            lo = (r1 - mid.astype(F32)).astype(BF16)
            acs = _dot(tril_ref[...], jnp.concatenate([hi, mid, lo], axis=0))
        elif g == 6:
            acsf_scr[...] = expand(acs)
        conv_group(g, pre)

    lane_head = lax.broadcasted_iota(jnp.int32, (1, GROUP_W), 1) // HEAD_DIM
    head_lanes = [(lane_head == r).astype(BF16) for r in range(GROUP_W // HEAD_DIM)]
    groups = range(SSM_GROUPS)
    gls = [slice(g * GROUP_W, (g + 1) * GROUP_W) for g in groups]

    def chunk_body(c, carry):
        r0 = pl.multiple_of(c * CHUNK, CHUNK)
        rows = pl.ds(r0, CHUNK)
        b_bf = [bc_scr[rows, g * D_STATE:(g + 1) * D_STATE] for g in groups]
        c_bf = [bc_scr[rows, (SSM_GROUPS + g) * D_STATE:(SSM_GROUPS + g + 1) * D_STATE] for g in groups]
        cb4 = [_dot_nt(c_bf[g], jnp.concatenate([b_bf[g]] * 4, axis=0)) for g in groups]

        dt_f = dtf_scr[rows, :]
        acs_f = acsf_scr[rows, :]
        y_off =[_dot(c_bf[g], h_scr[:, gls[g]].astype(BF16)) for g in groups]
        acs_t = jnp.sum(dt_f * a_full * triu_ref[...], axis=0, keepdims=True)
        acs_last = acs_f[CHUNK - 1:CHUNK, :]
        dec_chunk = jnp.exp(acs_last)
        xs = xs_scr[rows, :]
        xdt = xs * dt_f
        xdec = (xdt * jnp.exp(acs_last - acs_f)).astype(BF16)
        for g in groups:
            s_g = _dot_tn(b_bf[g], xdec[:, gls[g]])
            h_scr[:, gls[g]] = h_scr[:, gls[g]] * dec_chunk[:, gls[g]] + s_g

        lmat = jnp.exp(acs_f - acs_t + negmask_ref[...])
        dec_out = jnp.exp(acs_f)
        xdt_bf = xdt.astype(BF16)
        for g in groups:
            w = (cb4[g] * lmat[:, gls[g]]).astype(BF16)
            x_g = xdt_bf[:, gls[g]]
            bd = jnp.concatenate([x_g * m for m in head_lanes], axis=0)
            y_scr[rows, gls[g]] = (_dot(w, bd) + y_off[g] * dec_out[:, gls[g]]
                                   + dskip_ref[:, gls[g]] * xs[:, gls[g]])
        return carry

    lax.fori_loop(0, ts // CHUNK, chunk_body, 0, unroll=True)

    out = x
    half = D_INNER // 2
    for hf in range(2):
        parts = []
        for g in range(hf * SSM_GROUPS // 2, (hf + 1) * SSM_GROUPS // 2):
            gg = y_scr[:, gls[g]] * _silu(z_scr[:, gls[g]])
            parts.append(gg * lax.rsqrt(jnp.mean(gg * gg, axis=-1, keepdims=True) + EPS))
        cols = slice(hf * half, (hf + 1) * half)
        yn = (jnp.concatenate(parts, axis=1) * onorm_ref[:, cols]).astype(BF16)
        out = out + _dot(yn, wout_ref[cols, :])
    o_ref[0] = out


def _ssm_constants(ts):
    h_of_lane = jnp.arange(D_INNER) // HEAD_DIM
    e0 = (jnp.arange(LANES)[:, None] == h_of_lane[None, :]).astype(BF16)
    expand = jnp.concatenate([e0, e0], axis=0)
    t_lane = jnp.arange(D_INNER) % HEAD_DIM
    t_row = jnp.arange(CHUNK)
    triu = (t_row[:, None] <= t_lane[None, :]).astype(F32)
    negmask = jnp.where(t_lane[None, :] <= t_row[:, None], 0.0, -jnp.inf).astype(F32)
    r = jnp.arange(ts)
    tril = ((r[None, :] <= r[:, None]) & (r[None, :] // CHUNK == r[:, None] // CHUNK)).astype(BF16)
    return expand, triu, negmask, jnp.concatenate([tril] * 3, axis=1)


def _ssm(x, g, w_in, conv_w, conv_b, dt_bias, a_log, d_skip, out_norm, w_out):
    b, s, _ = x.shape
    ts = min(SSM_ROWS, s)
    wdt = jnp.pad(w_in[:, D_INNER + CONV_DIM:], ((0, 0), (0, LANES - SSM_HEADS))).astype(BF16)
    pad_h = lambda v: jnp.pad(v.reshape(1, SSM_HEADS), ((0, 0), (0, LANES - SSM_HEADS)))
    rep = lambda v: jnp.repeat(v, HEAD_DIM).reshape(1, D_INNER)
    expand, triu, negmask, tril = _ssm_constants(ts)
    operands = (
        x, g.reshape(1, D_MODEL), w_in.astype(BF16), wdt, conv_w, conv_b.reshape(1, CONV_DIM),
        pad_h(dt_bias), pad_h(a_log), rep(a_log), rep(d_skip), out_norm.reshape(1, D_INNER),
        w_out.astype(BF16), expand, triu, negmask, tril)
    in_specs = [pl.BlockSpec((1, ts, D_MODEL), lambda i, j: (i, j, 0))]
    in_specs += [_resident(op.shape) for op in operands[1:]]
    return pl.pallas_call(
        _ssm_kernel,
        grid=(b, s // ts),
        in_specs=in_specs,
        out_specs=pl.BlockSpec((1, ts, D_MODEL), lambda i, j: (i, j, 0)),
        out_shape=jax.ShapeDtypeStruct((b, s, D_MODEL), F32),
        scratch_shapes=[
            pltpu.VMEM((D_STATE, D_INNER), F32),
            pltpu.VMEM((8, CONV_DIM), F32),
            pltpu.VMEM((2, ts + 8, CONV_COLS), F32),
            pltpu.VMEM((ts, D_INNER), F32),
            pltpu.VMEM((ts, CONV_DIM - D_INNER), BF16),
            pltpu.VMEM((ts, D_INNER), F32),
            pltpu.VMEM((ts, D_INNER), F32),
            pltpu.VMEM((ts, D_INNER), F32),
            pltpu.VMEM((ts, D_INNER), F32),
        ],
        compiler_params=_params("arbitrary", "arbitrary"),
        name="ssm",
    )(*operands)


def _head_norm_constants(width=ATT_DIM):
    c = jnp.arange(width) // HEAD_DIM
    s0 = (c[:, None] == jnp.arange(LANES)[None, :]).astype(BF16)
    return jnp.concatenate([s0, s0], axis=0), jnp.concatenate([s0.T, s0.T], axis=0)


def _head_norm(v, sum_ref, spread_ref):
    hi, lo = _split2(v * v)
    ms = _dot(jnp.concatenate([hi, lo], axis=1), sum_ref[...]) * (1.0 / HEAD_DIM)
    rh, rl = _split2(lax.rsqrt(ms + EPS))
    return v * _dot(jnp.concatenate([rh, rl], axis=1), spread_ref[...])


def _kv_kernel(x_ref, g_ref, wk_ref, wvt_ref, kn_ref, sum_ref, spread_ref, k_ref, vt_ref):
    i = pl.program_id(1)

    @pl.when(i == 0)
    def _():
        k_ref[...] = jnp.zeros_like(k_ref)
        vt_ref[...] = jnp.zeros_like(vt_ref)

    @pl.when(i > 0)
    def _():
        xn = _rms(x_ref[0], g_ref[...]).astype(BF16)
        k = _head_norm(_dot(xn, wk_ref[...]), sum_ref, spread_ref) * kn_ref[...]
        k_ref[0] = k.astype(BF16)
        vt_ref[0] = _dot_nt(wvt_ref[...], xn).astype(BF16)


def _shared_kv(x, g, w_kv, k_norm):
    b, s, _ = x.shape
    tk = PAD_LEN
    hsum, hspread = _head_norm_constants()
    return pl.pallas_call(
        _kv_kernel,
        grid=(b, s // tk + 1),
        in_specs=[
            pl.BlockSpec((1, tk, D_MODEL), lambda i, j: (i, jnp.maximum(j - 1, 0), 0)),
            _resident((1, D_MODEL)),
            _resident((D_MODEL, ATT_DIM)),
            _resident((ATT_DIM, D_MODEL)),
            _resident((1, ATT_DIM)),
            _resident(hsum.shape),
            _resident(hspread.shape),
        ],
        out_specs=[pl.BlockSpec((1, tk, ATT_DIM), lambda i, j: (i, j, 0)),
                   pl.BlockSpec((1, ATT_DIM, tk), lambda i, j: (i, 0, j))],
        out_shape=[jax.ShapeDtypeStruct((b, s + PAD_LEN, ATT_DIM), BF16),
                   jax.ShapeDtypeStruct((b, ATT_DIM, s + PAD_LEN), BF16)],
        compiler_params=_params("arbitrary", "arbitrary"),
        name="shared_kv",
    )(x, g.reshape(1, D_MODEL), w_kv[:, :ATT_DIM].astype(BF16), w_kv[:, ATT_DIM:].T.astype(BF16),
      jnp.tile(k_norm, ATT_HEADS).reshape(1, ATT_DIM), hsum, hspread)


def _bias_kernel(w_ref, o_ref):
    rows = jnp.broadcast_to(w_ref[0], (NEAR_ROWS, TOEP_W))
    toep = pltpu.roll(rows, 0, 1, stride=1, stride_axis=0)[:, 0:ATT_ROWS]
    key_chunk = lax.broadcasted_iota(jnp.int32, (NEAR_ROWS, ATT_ROWS), 0) // CHUNK + (ATT_WIN - NEAR_ROWS) // CHUNK
    query_chunk = lax.broadcasted_iota(jnp.int32, (NEAR_ROWS, ATT_ROWS), 1) // CHUNK
    o_ref[0] = jnp.where(key_chunk - query_chunk <= LEFT_CHUNKS, toep, -jnp.inf)


def _rel_bias_tables(rel_bias):
    far = rel_bias[:, :1]
    w = jnp.concatenate([
        jnp.broadcast_to(far, (ATT_HEADS, ATT_ROWS + 1)),
        jnp.broadcast_to(rel_bias[:, -1:], (ATT_HEADS, TOEP_W - ATT_ROWS - 1 - 2 * MAX_REL)),
        rel_bias[:, :0:-1],
    ], axis=1).reshape(ATT_HEADS, 1, TOEP_W)
    near = pl.pallas_call(
        _bias_kernel,
        grid=(ATT_HEADS,),
        in_specs=[pl.BlockSpec((1, 1, TOEP_W), lambda h: (h, 0, 0))],
        out_specs=pl.BlockSpec((1, NEAR_ROWS, ATT_ROWS), lambda h: (h, 0, 0)),
        out_shape=jax.ShapeDtypeStruct((ATT_HEADS, NEAR_ROWS, ATT_ROWS), F32),
        compiler_params=_params("arbitrary"),
        name="rel_bias",
    )(w)
    return near, jnp.broadcast_to(far[:, :, None], (ATT_HEADS, 1, ATT_ROWS))


def _attn_layer_kernel(x_ref, g_ref, wq_ref, qn_ref, sum_ref, spread_ref, k_ref, vt_ref,
                       near_ref, far_ref, farmask_ref, wo_ref, o_ref, q_scr, ot_scr):
    base = pl.multiple_of(pl.program_id(1) * ATT_ROWS, ATT_ROWS)
    x = x_ref[0]
    xn = _rms(x, g_ref[...]).astype(BF16)

    def q_project(g):
        return _dot(xn, wq_ref[:, g * Q_COLS:(g + 1) * Q_COLS])

    def q_mean_sq(qraw):
        hi, lo = _split2(qraw * qraw)
        return _dot(jnp.concatenate([hi, lo], axis=1), sum_ref[...]) * (1.0 / HEAD_DIM)

    def q_inv_rms(ms):
        rh, rl = _split2(lax.rsqrt(ms + EPS))
        return _dot(jnp.concatenate([rh, rl], axis=1), spread_ref[...])

    def q_store(g, qraw, inv):
        cols = slice(g * Q_COLS, (g + 1) * Q_COLS)
        q_scr[:, cols] = (qraw * inv * qn_ref[:, cols]).astype(BF16)

    qraw = q_project(0)
    q_store(0, qraw, q_inv_rms(q_mean_sq(qraw)))

    lane = lax.broadcasted_iota(jnp.int32, (1, LANES), 1)
    head_mask = [(lane < HEAD_DIM).astype(BF16), (lane >= HEAD_DIM).astype(BF16)]
    n_kc = ATT_WIN // CHUNK
    far_kc = (ATT_WIN - NEAR_ROWS) // CHUNK
    pad_neg = [jnp.where(base + kc * CHUNK < PAD_LEN, -jnp.inf, 0.0).astype(F32)
               for kc in range(PAD_LEN // CHUNK)]

    def scores_t(h):
        l0 = (h // 2) * LANES
        q_pair = q_scr[:, pl.ds(l0, LANES)]
        k_win = k_ref[0, pl.ds(base, ATT_WIN), pl.ds(l0, LANES)]
        return _dot_nt(k_win, q_pair * head_mask[h % 2])

    heads_per_group = Q_COLS // HEAD_DIM
    half_dim = ATT_DIM // 2
    out = x
    qraw = q_project(1)
    pending = [scores_t(h) for h in range(SCORE_LOOKAHEAD)]
    for h in range(ATT_HEADS):
        st = pending.pop(0)
        g_next, stage = divmod(h + SCORE_LOOKAHEAD + heads_per_group - 1, heads_per_group)
        if 1 <= g_next < ATT_DIM // Q_COLS:
            if stage == 0:
                qraw = q_project(g_next)
            elif stage == 1:
                ms = q_mean_sq(qraw)
            elif stage == 2:
                inv = q_inv_rms(ms)
            else:
                q_store(g_next, qraw, inv)
        if h + SCORE_LOOKAHEAD < ATT_HEADS:
            pending.append(scores_t(h + SCORE_LOOKAHEAD))
        far_row = far_ref[h]
        blocks = []
        for kc in range(n_kc):
            blk = st[kc * CHUNK:(kc + 1) * CHUNK, :]
            if kc < far_kc:
                blk = blk + (far_row + pad_neg[kc])
                if kc < FAR_MASK_CHUNKS:
                    blk = blk + farmask_ref[kc * CHUNK:(kc + 1) * CHUNK, :]
            else:
                blk = blk + near_ref[h, (kc - far_kc) * CHUNK:(kc - far_kc + 1) * CHUNK, :]
                if kc < len(pad_neg):
                    blk = blk + pad_neg[kc]
            blocks.append(blk)
        s = jnp.concatenate(blocks, axis=0)
        p = jnp.exp2(s - jnp.max(s, axis=0, keepdims=True))
        denom = jnp.sum(p, axis=0, keepdims=True)
        r0 = h * HEAD_DIM
        vt = vt_ref[0, pl.ds(r0, HEAD_DIM), pl.ds(base, ATT_WIN)]
        ot = _dot(vt, p.astype(BF16)) / denom
        ot_scr[pl.ds(r0, HEAD_DIM), :] = ot.astype(BF16)
        if h == ATT_HEADS // 2:
            out = out + _dot_tn(ot_scr[0:half_dim, :], wo_ref[0:half_dim, :])
    o_ref[0] = out + _dot_tn(ot_scr[half_dim:, :], wo_ref[half_dim:, :])


def _attn_layer(x, k_pad, vt_pad, g, w_q, q_norm, rel_bias, w_o):
    b, s, _ = x.shape
    hsum, hspread = _head_norm_constants(Q_COLS)
    log2e = 1.0 / math.log(2.0)
    near, far = _rel_bias_tables(rel_bias * log2e)
    key_chunk = jnp.arange(FAR_MASK_CHUNKS * CHUNK) // CHUNK
    query_chunk = jnp.arange(ATT_ROWS) // CHUNK
    farmask = jnp.where(key_chunk[:, None] >= query_chunk[None, :], 0.0, -jnp.inf).astype(F32)
    qn = jnp.tile(q_norm, ATT_HEADS).reshape(1, ATT_DIM) * (log2e / math.sqrt(HEAD_DIM))
    return pl.pallas_call(
        _attn_layer_kernel,
        grid=(b, s // ATT_ROWS),
        in_specs=[
            pl.BlockSpec((1, ATT_ROWS, D_MODEL), lambda i, j: (i, j, 0)),
            _resident((1, D_MODEL)),
            _resident((D_MODEL, ATT_DIM)),
            _resident((1, ATT_DIM)),
            _resident(hsum.shape),
            _resident(hspread.shape),
            pl.BlockSpec((1, s + PAD_LEN, ATT_DIM), lambda i, j: (i, 0, 0), pipeline_mode=pl.Buffered(1)),
            pl.BlockSpec((1, ATT_DIM, s + PAD_LEN), lambda i, j: (i, 0, 0), pipeline_mode=pl.Buffered(1)),
            _resident(near.shape),
            _resident(far.shape),
            _resident(farmask.shape),
            _resident((ATT_DIM, D_MODEL)),
        ],
        out_specs=pl.BlockSpec((1, ATT_ROWS, D_MODEL), lambda i, j: (i, j, 0)),
        out_shape=jax.ShapeDtypeStruct((b, s, D_MODEL), F32),
        scratch_shapes=[
            pltpu.VMEM((ATT_ROWS, ATT_DIM), BF16),
            pltpu.VMEM((ATT_DIM, ATT_ROWS), BF16),
        ],
        compiler_params=_params("arbitrary", "arbitrary"),
        name="attn_layer",
    )(x, g.reshape(1, D_MODEL), w_q.astype(BF16), qn, hsum, hspread, k_pad, vt_pad,
      near, far, farmask, w_o.astype(BF16))


def kernel(x, ffn1_norm, ffn1_w_gate, ffn1_w_up, ffn1_w_down, ffn2_norm, ffn2_w_gate, ffn2_w_up,
           ffn2_w_down, ssm_norm, ssm_in_proj, ssm_conv_w, ssm_conv_b, ssm_dt_bias, ssm_A_log, ssm_D,
           ssm_out_norm, ssm_out_proj, kv_norm, w_kv, k_norm, att_norm, att_w_q, att_q_norm,
           att_rel_bias, att_w_o):
    b, s, d = x.shape
    depth = ffn1_norm.shape[0]
    n_ssm = ssm_norm.shape[0]
    t = b * s
    k_pad = vt_pad = None
    for l in range(depth):
        x = _ffn(x.reshape(t, d), ffn1_norm[l], ffn1_w_gate[l].astype(BF16),
                 ffn1_w_up[l].astype(BF16), ffn1_w_down[l].astype(BF16)).reshape(b, s, d)
        if l < n_ssm:
            x = _ssm(x, ssm_norm[l], ssm_in_proj[l], ssm_conv_w[l], ssm_conv_b[l], ssm_dt_bias[l],
                     ssm_A_log[l], ssm_D[l], ssm_out_norm[l], ssm_out_proj[l])
        else:
            if l == n_ssm:
                k_pad, vt_pad = _shared_kv(x, kv_norm, w_kv, k_norm)
            j = l - n_ssm
            x = _attn_layer(x, k_pad, vt_pad, att_norm[j], att_w_q[j], att_q_norm[j],
                            att_rel_bias[j], att_w_o[j])
        x = _ffn(x.reshape(t, d), ffn2_norm[l], ffn2_w_gate[l].astype(BF16),
                 ffn2_w_up[l].astype(BF16), ffn2_w_down[l].astype(BF16)).reshape(b, s, d)
    return x
```

```python
import functools
import math

import jax
import jax.numpy as jnp
from jax import lax
from jax.experimental import pallas as pl
from jax.experimental.pallas import tpu as pltpu

F32 = jnp.float32
BF16 = jnp.bfloat16

D_MODEL = 1024
D_FF = 2816
EPS = 1e-6

D_INNER = 2048
HEAD_DIM = 64
SSM_HEADS = 32
SSM_GROUPS = 8
D_STATE = 128
CONV_W = 4
CONV_DIM = D_INNER + 2 * SSM_GROUPS * D_STATE
GROUP_W = D_INNER // SSM_GROUPS
CHUNK = 64
CONV_COLS = 512

ATT_HEADS = 16
ATT_DIM = 1024
LEFT_CHUNKS = 8
BAND = (LEFT_CHUNKS + 1) * CHUNK
PAD_LEN = LEFT_CHUNKS * CHUNK
MAX_REL = 128

LANES = 128
VMEM_LIMIT = 56 * 1024 * 1024

FFN_ROWS = 512
SSM_ROWS = 256
Q_COLS = 256
SCORE_LOOKAHEAD = 2
ATT_ROWS = 256
ATT_WIN = ATT_ROWS + PAD_LEN
NEAR_ROWS = ATT_WIN - (PAD_LEN - MAX_REL)
FAR_MASK_CHUNKS = ATT_ROWS // CHUNK - 1
TOEP_W = 640


def _resident(shape):
    nd = len(shape)
    return pl.BlockSpec(shape, lambda *_: (0,) * nd, pipeline_mode=pl.Buffered(1))


def _params(*sem):
    return pltpu.CompilerParams(dimension_semantics=sem, vmem_limit_bytes=VMEM_LIMIT)


def _rms(x, g):
    return x * lax.rsqrt(jnp.mean(x * x, axis=-1, keepdims=True) + EPS) * g


def _dot(a, b):
    return jnp.dot(a, b, preferred_element_type=F32)


def _dot_nt(a, b):
    return lax.dot_general(a, b, (((1,), (1,)), ((), ())), preferred_element_type=F32)


def _dot_tn(a, b):
    return lax.dot_general(a, b, (((0,), (0,)), ((), ())), preferred_element_type=F32)


def _split2(v):
    hi = v.astype(BF16)
    lo = (v - hi.astype(F32)).astype(BF16)
    return hi, lo


def _silu(v):
    return v * jax.nn.sigmoid(v)


def _ffn_kernel(x_ref, g_ref, wg_ref, wu_ref, wd_ref, o_ref):
    half = x_ref.shape[0] // 2
    hidden = []
    for r in range(2):
        x = x_ref[r * half:(r + 1) * half, :]
        xn = _rms(x, g_ref[...]).astype(BF16)
        hidden.append((_dot(xn, wg_ref[...]), _dot(xn, wu_ref[...])))
    for r in range(2):
        gate, up = hidden[r]
        h = (_silu(gate) * up).astype(BF16)
        rows = slice(r * half, (r + 1) * half)
        o_ref[rows, :] = x_ref[rows, :] + 0.5 * _dot(h, wd_ref[...])


def _ffn(x2, g, wg, wu, wd):
    t = x2.shape[0]
    tm = min(FFN_ROWS, t)
    return pl.pallas_call(
        _ffn_kernel,
        grid=(t // tm,),
        in_specs=[
            pl.BlockSpec((tm, D_MODEL), lambda i: (i, 0)),
            _resident((1, D_MODEL)),
            _resident((D_MODEL, D_FF)),
            _resident((D_MODEL, D_FF)),
            _resident((D_FF, D_MODEL)),
        ],
        out_specs=pl.BlockSpec((tm, D_MODEL), lambda i: (i, 0)),
        out_shape=jax.ShapeDtypeStruct((t, D_MODEL), F32),
        compiler_params=_params("arbitrary"),
        name="ffn",
    )(x2, g.reshape(1, D_MODEL), wg, wu, wd)


def _ssm_kernel(x_ref, g_ref, win_ref, wdt_ref, cw_ref, cb_ref, dtb_ref,
                alog_ref, alogf_ref, dskip_ref, onorm_ref, wout_ref,
                expand_ref, triu_ref, negmask_ref, tril_ref, o_ref,
                h_scr, tail_scr, xpad_scr, xs_scr, bc_scr, z_scr, dtf_scr, acsf_scr, y_scr):
    ts = x_ref.shape[1]

    @pl.when(pl.program_id(1) == 0)
    def _():
        h_scr[...] = jnp.zeros_like(h_scr)
        tail_scr[...] = jnp.zeros_like(tail_scr)

    x = x_ref[0]
    xn = _rms(x, g_ref[...]).astype(BF16)

    n_grp = CONV_DIM // CONV_COLS
    n_xs = D_INNER // CONV_COLS

    def project(g):
        return _dot(xn, win_ref[:, D_INNER + g * CONV_COLS:D_INNER + (g + 1) * CONV_COLS])

    def conv_group(g, pre):
        cols = slice(g * CONV_COLS, (g + 1) * CONV_COLS)
        slot = g % 2
        xpad_scr[slot, 0:8, :] = tail_scr[:, cols]
        xpad_scr[slot, 8:, :] = pre
        tail_scr[:, cols] = pre[ts - 8:, :]
        conv = cb_ref[:, cols] + cw_ref[CONV_W - 1:CONV_W, cols] * pre
        for k in range(CONV_W - 1):
            off = 8 - (CONV_W - 1) + k
            conv = conv + cw_ref[k:k + 1, cols] * xpad_scr[slot, off:off + ts, :]
        act = _silu(conv)
        if g < n_xs:
            xs_scr[:, cols] = act
        else:
            bc_scr[:, (g - n_xs) * CONV_COLS:(g - n_xs + 1) * CONV_COLS] = act.astype(BF16)

    a_head = -jnp.exp(alog_ref[...])
    a_full = -jnp.exp(alogf_ref[...])

    def expand(v):
        hi, lo = _split2(v)
        return _dot(jnp.concatenate([hi, lo], axis=1), expand_ref[...])

    pre_next = project(0)
    for g in range(n_grp):
        pre = pre_next
        if g + 1 < n_grp:
            pre_next = project(g + 1)
        if g % 2 == 1:
            zc = slice((g // 2) * CONV_COLS, (g // 2 + 1) * CONV_COLS)
            z_scr[:, zc] = _dot(xn, win_ref[:, zc])
        if g == 0:
            dt = jax.nn.softplus(_dot(xn, wdt_ref[...]) + dtb_ref[...])
        elif g == 2:
            dtf_scr[...] = expand(dt)
        elif g == 4:
            a = dt * a_head
            hi = a.astype(BF16)
            r1 = a - hi.astype(F32)
            mid = r1.astype(BF16)
            lo = (r1 - mid.astype(F32)).astype(BF16)
            acs = _dot(tril_ref[...], jnp.concatenate([hi, mid, lo], axis=0))
        elif g == 6:
            acsf_scr[...] = expand(acs)
        conv_group(g, pre)

    lane_head = lax.broadcasted_iota(jnp.int32, (1, GROUP_W), 1) // HEAD_DIM
    head_lanes = [(lane_head == r).astype(BF16) for r in range(GROUP_W // HEAD_DIM)]
    groups = range(SSM_GROUPS)
    gls = [slice(g * GROUP_W, (g + 1) * GROUP_W) for g in groups]

    def chunk_body(c, carry):
        r0 = pl.multiple_of(c * CHUNK, CHUNK)
        rows = pl.ds(r0, CHUNK)
        b_bf = [bc_scr[rows, g * D_STATE:(g + 1) * D_STATE] for g in groups]
        c_bf = [bc_scr[rows, (SSM_GROUPS + g) * D_STATE:(SSM_GROUPS + g + 1) * D_STATE] for g in groups]
        cb4 = [_dot_nt(c_bf[g], jnp.concatenate([b_bf[g]] * 4, axis=0)) for g in groups]

        y_off = [_dot(c_bf[g], h_scr[:, gls[g]].astype(BF16)) for g in groups]
        acs_last = acsf_scr[pl.ds(r0 + CHUNK - 1, 1), :]
        dec_chunk = jnp.exp(acs_last)
        for g in groups:
            gl = gls[g]
            acs_g = acsf_scr[rows, gl]
            xdec = (xs_scr[rows, gl] * dtf_scr[rows, gl] * jnp.exp(acs_last[:, gl] - acs_g)).astype(BF16)
            s_g = _dot_tn(b_bf[g], xdec)
            h_scr[:, gl] = h_scr[:, gl] * dec_chunk[:, gl] + s_g

        for g in groups:
            gl = gls[g]
            acs_g = acsf_scr[rows, gl]
            dt_g = dtf_scr[rows, gl]
            xs_g = xs_scr[rows, gl]
            acs_t = jnp.sum(dt_g * a_full[:, gl] * triu_ref[:, gl], axis=0, keepdims=True)
            lmat = jnp.exp(acs_g - acs_t + negmask_ref[:, gl])
            w = (cb4[g] * lmat).astype(BF16)
            x_g = (xs_g * dt_g).astype(BF16)
            bd = jnp.concatenate([x_g * m for m in head_lanes], axis=0)
            y_scr[rows, gl] = (_dot(w, bd) + y_off[g] * jnp.exp(acs_g) + dskip_ref[:, gl] * xs_g)
        return carry

    lax.fori_loop(0, ts // CHUNK, chunk_body, 0, unroll=True)

    out = x
    half = D_INNER // 2
    for hf in range(2):
        parts = []
        for g in range(hf * SSM_GROUPS // 2, (hf + 1) * SSM_GROUPS // 2):
            gg = y_scr[:, gls[g]] * _silu(z_scr[:, gls[g]])
            parts.append(gg * lax.rsqrt(jnp.mean(gg * gg, axis=-1, keepdims=True) + EPS))
        cols = slice(hf * half, (hf + 1) * half)
        yn = (jnp.concatenate(parts, axis=1) * onorm_ref[:, cols]).astype(BF16)
        out = out + _dot(yn, wout_ref[cols, :])
    o_ref[0] = out


def _ssm_constants(ts):
    h_of_lane = jnp.arange(D_INNER) // HEAD_DIM
    e0 = (jnp.arange(LANES)[:, None] == h_of_lane[None, :]).astype(BF16)
    expand = jnp.concatenate([e0, e0], axis=0)
    t_lane = jnp.arange(D_INNER) % HEAD_DIM
    t_row = jnp.arange(CHUNK)
    triu = (t_row[:, None] <= t_lane[None, :]).astype(F32)
    negmask = jnp.where(t_lane[None, :] <= t_row[:, None], 0.0, -jnp.inf).astype(F32)
    r = jnp.arange(ts)
    tril = ((r[None, :] <= r[:, None]) & (r[None, :] // CHUNK == r[:, None] // CHUNK)).astype(BF16)
    return expand, triu, negmask, jnp.concatenate([tril] * 3, axis=1)


def _ssm(x, g, w_in, conv_w, conv_b, dt_bias, a_log, d_skip, out_norm, w_out):
    b, s, _ = x.shape
    ts = min(SSM_ROWS, s)
    wdt = jnp.pad(w_in[:, D_INNER + CONV_DIM:], ((0, 0), (0, LANES - SSM_HEADS))).astype(BF16)
    pad_h = lambda v: jnp.pad(v.reshape(1, SSM_HEADS), ((0, 0), (0, LANES - SSM_HEADS)))
    rep = lambda v: jnp.repeat(v, HEAD_DIM).reshape(1, D_INNER)
    expand, triu, negmask, tril = _ssm_constants(ts)
    operands = (
        x, g.reshape(1, D_MODEL), w_in.astype(BF16), wdt, conv_w, conv_b.reshape(1, CONV_DIM),
        pad_h(dt_bias), pad_h(a_log), rep(a_log), rep(d_skip), out_norm.reshape(1, D_INNER),
        w_out.astype(BF16), expand, triu, negmask, tril)
    in_specs = [pl.BlockSpec((1, ts, D_MODEL), lambda i, j: (i, j, 0))]
    in_specs += [_resident(op.shape) for op in operands[1:]]
    return pl.pallas_call(
        _ssm_kernel,
        grid=(b, s // ts),
        in_specs=in_specs,
        out_specs=pl.BlockSpec((1, ts, D_MODEL), lambda i, j: (i, j, 0)),
        out_shape=jax.ShapeDtypeStruct((b, s, D_MODEL), F32),
        scratch_shapes=[
            pltpu.VMEM((D_STATE, D_INNER), F32),
            pltpu.VMEM((8, CONV_DIM), F32),
            pltpu.VMEM((2, ts + 8, CONV_COLS), F32),
            pltpu.VMEM((ts, D_INNER), F32),
            pltpu.VMEM((ts, CONV_DIM - D_INNER), BF16),
            pltpu.VMEM((ts, D_INNER), F32),
            pltpu.VMEM((ts, D_INNER), F32),
            pltpu.VMEM((ts, D_INNER), F32),
            pltpu.VMEM((ts, D_INNER), F32),
        ],
        compiler_params=_params("arbitrary", "arbitrary"),
        name="ssm",
    )(*operands)


def _head_norm_constants(width=ATT_DIM):
    c = jnp.arange(width) // HEAD_DIM
    s0 = (c[:, None] == jnp.arange(LANES)[None, :]).astype(BF16)
    return jnp.concatenate([s0, s0], axis=0), jnp.concatenate([s0.T, s0.T], axis=0)


def _head_norm(v, sum_ref, spread_ref):
    hi, lo = _split2(v * v)
    ms = _dot(jnp.concatenate([hi, lo], axis=1), sum_ref[...]) * (1.0 / HEAD_DIM)
    rh, rl = _split2(lax.rsqrt(ms + EPS))
    return v * _dot(jnp.concatenate([rh, rl], axis=1), spread_ref[...])


def _kv_kernel(x_ref, g_ref, wk_ref, wvt_ref, kn_ref, sum_ref, spread_ref, k_ref, vt_ref):
    i = pl.program_id(1)

    @pl.when(i == 0)
    def _():
        k_ref[...] = jnp.zeros_like(k_ref)
        vt_ref[...] = jnp.zeros_like(vt_ref)

    @pl.when(i > 0)
    def _():
        xn = _rms(x_ref[0], g_ref[...]).astype(BF16)
        k = _head_norm(_dot(xn, wk_ref[...]), sum_ref, spread_ref) * kn_ref[...]
        k_ref[0] = k.astype(BF16)
        vt_ref[0] = _dot_nt(wvt_ref[...], xn).astype(BF16)


def _shared_kv(x, g, w_kv, k_norm):
    b, s, _ = x.shape
    tk = PAD_LEN
    hsum, hspread = _head_norm_constants()
    return pl.pallas_call(
        _kv_kernel,
        grid=(b, s // tk + 1),
        in_specs=[
            pl.BlockSpec((1, tk, D_MODEL), lambda i, j: (i, jnp.maximum(j - 1, 0), 0)),
            _resident((1, D_MODEL)),
            _resident((D_MODEL, ATT_DIM)),
            _resident((ATT_DIM, D_MODEL)),
            _resident((1, ATT_DIM)),
            _resident(hsum.shape),
            _resident(hspread.shape),
        ],
        out_specs=[pl.BlockSpec((1, tk, ATT_DIM), lambda i, j: (i, j, 0)),
                   pl.BlockSpec((1, ATT_DIM, tk), lambda i, j: (i, 0, j))],
        out_shape=[jax.ShapeDtypeStruct((b, s + PAD_LEN, ATT_DIM), BF16),
                   jax.ShapeDtypeStruct((b, ATT_DIM, s + PAD_LEN), BF16)],
        compiler_params=_params("arbitrary", "arbitrary"),
        name="shared_kv",
    )(x, g.reshape(1, D_MODEL), w_kv[:, :ATT_DIM].astype(BF16), w_kv[:, ATT_DIM:].T.astype(BF16),
      jnp.tile(k_norm, ATT_HEADS).reshape(1, ATT_DIM), hsum, hspread)


def _bias_kernel(w_ref, o_ref):
    rows = jnp.broadcast_to(w_ref[0], (NEAR_ROWS, TOEP_W))
    toep = pltpu.roll(rows, 0, 1, stride=1, stride_axis=0)[:, 0:ATT_ROWS]
    key_chunk = lax.broadcasted_iota(jnp.int32, (NEAR_ROWS, ATT_ROWS), 0) // CHUNK + (ATT_WIN - NEAR_ROWS) // CHUNK
    query_chunk = lax.broadcasted_iota(jnp.int32, (NEAR_ROWS, ATT_ROWS), 1) // CHUNK
    o_ref[0] = jnp.where(key_chunk - query_chunk <= LEFT_CHUNKS, toep, -jnp.inf)


def _rel_bias_tables(rel_bias):
    far = rel_bias[:, :1]
    w = jnp.concatenate([
        jnp.broadcast_to(far, (ATT_HEADS, ATT_ROWS + 1)),
        jnp.broadcast_to(rel_bias[:, -1:], (ATT_HEADS, TOEP_W - ATT_ROWS - 1 - 2 * MAX_REL)),
        rel_bias[:, :0:-1],
    ], axis=1).reshape(ATT_HEADS, 1, TOEP_W)
    near = pl.pallas_call(
        _bias_kernel,
        grid=(ATT_HEADS,),
        in_specs=[pl.BlockSpec((1, 1, TOEP_W), lambda h: (h, 0, 0))],
        out_specs=pl.BlockSpec((1, NEAR_ROWS, ATT_ROWS), lambda h: (h, 0, 0)),
        out_shape=jax.ShapeDtypeStruct((ATT_HEADS, NEAR_ROWS, ATT_ROWS), F32),
        compiler_params=_params("arbitrary"),
        name="rel_bias",
    )(w)
    return near, jnp.broadcast_to(far[:, :, None], (ATT_HEADS, 1, ATT_ROWS))


def _attn_layer_kernel(x_ref, g_ref, wq_ref, qn_ref, sum_ref, spread_ref, k_ref, vt_ref,
                       near_ref, far_ref, farmask_ref, wo_ref, o_ref, q_scr, ot_scr):
    base = pl.multiple_of(pl.program_id(1) * ATT_ROWS, ATT_ROWS)
    x = x_ref[0]
    xn = _rms(x, g_ref[...]).astype(BF16)

    def q_project(g):
        return _dot(xn, wq_ref[:, g * Q_COLS:(g + 1) * Q_COLS])

    def q_mean_sq(qraw):
        hi, lo = _split2(qraw * qraw)
        return _dot(jnp.concatenate([hi, lo], axis=1), sum_ref[...]) * (1.0 / HEAD_DIM)

    def q_inv_rms(ms):
        rh, rl = _split2(lax.rsqrt(ms + EPS))
        return _dot(jnp.concatenate([rh, rl], axis=1), spread_ref[...])

    def q_store(g, qraw, inv):
        cols = slice(g * Q_COLS, (g + 1) * Q_COLS)
        q_scr[:, cols] = (qraw * inv * qn_ref[:, cols]).astype(BF16)

    qraw = q_project(0)
    q_store(0, qraw, q_inv_rms(q_mean_sq(qraw)))

    lane = lax.broadcasted_iota(jnp.int32, (1, LANES), 1)
    head_mask = [(lane < HEAD_DIM).astype(BF16), (lane >= HEAD_DIM).astype(BF16)]
    n_kc = ATT_WIN // CHUNK
    far_kc = (ATT_WIN - NEAR_ROWS) // CHUNK
    pad_neg = [jnp.where(base + kc * CHUNK < PAD_LEN, -jnp.inf, 0.0).astype(F32)
               for kc in range(PAD_LEN // CHUNK)]

    chunks_per_half = LANES // CHUNK
    seen_kc = LEFT_CHUNKS + chunks_per_half

    def scores_t(h):
        l0 = (h // 2) * LANES
        q_pair = q_scr[:, pl.ds(l0, LANES)]
        k_win = k_ref[0, pl.ds(base, ATT_WIN), pl.ds(l0, LANES)]
        st = _dot_nt(k_win, q_pair * head_mask[h % 2])
        far_row = far_ref[h]
        halves = []
        for hf in range(ATT_ROWS // LANES):
            lanes = slice(hf * LANES, (hf + 1) * LANES)
            first_kc = hf * chunks_per_half
            blocks = []
            for kc in range(first_kc, first_kc + seen_kc):
                rows = slice(kc * CHUNK, (kc + 1) * CHUNK)
                blk = st[rows, lanes]
                if kc < far_kc:
                    blk = blk + (far_row[:, lanes] + pad_neg[kc])
                    if kc == first_kc:
                        blk = blk + farmask_ref[rows, lanes]
                else:
                    blk = blk + near_ref[h, (kc - far_kc) * CHUNK:(kc - far_kc + 1) * CHUNK, lanes]
                    if kc < len(pad_neg):
                        blk = blk + pad_neg[kc]
                blocks.append(blk)
            s = jnp.concatenate(blocks, axis=0)
            halves.append((s, jnp.max(s, axis=0, keepdims=True)))
        return halves

    heads_per_group = Q_COLS // HEAD_DIM
    half_dim = ATT_DIM // 2
    out = x
    qraw = q_project(1)
    pending = [scores_t(h) for h in range(SCORE_LOOKAHEAD)]
    for h in range(ATT_HEADS):
        halves = pending.pop(0)
        g_next, stage = divmod(h + SCORE_LOOKAHEAD + heads_per_group - 1, heads_per_group)
        if 1 <= g_next < ATT_DIM // Q_COLS:
            if stage == 0:
                qraw = q_project(g_next)
            elif stage == 1:
                ms = q_mean_sq(qraw)
            elif stage == 2:
                inv = q_inv_rms(ms)
            else:
                q_store(g_next, qraw, inv)
        if h + SCORE_LOOKAHEAD < ATT_HEADS:
            pending.append(scores_t(h + SCORE_LOOKAHEAD))
        p_cols, denoms = [], []
        for hf, (s, s_max) in enumerate(halves):
            p_half = jnp.exp2(s - s_max)
            denoms.append(jnp.sum(p_half, axis=0, keepdims=True))
            above = jnp.zeros((hf * chunks_per_half * CHUNK, LANES), BF16)
            below = jnp.zeros((ATT_WIN - (hf * chunks_per_half + seen_kc) * CHUNK, LANES), BF16)
            p_cols.append(jnp.concatenate(
                [blk for blk in (above, p_half.astype(BF16), below) if blk.shape[0]], axis=0))
        p = jnp.concatenate(p_cols, axis=1)
        denom = jnp.concatenate(denoms, axis=1)
        r0 = h * HEAD_DIM
        vt = vt_ref[0, pl.ds(r0, HEAD_DIM), pl.ds(base, ATT_WIN)]
        ot = _dot(vt, p) / denom
        ot_scr[pl.ds(r0, HEAD_DIM), :] = ot.astype(BF16)
        if h == ATT_HEADS // 2:
            out = out + _dot_tn(ot_scr[0:half_dim, :], wo_ref[0:half_dim, :])
    o_ref[0] = out + _dot_tn(ot_scr[half_dim:, :], wo_ref[half_dim:, :])


def _attn_layer(x, k_pad, vt_pad, g, w_q, q_norm, rel_bias, w_o):
    b, s, _ = x.shape
    hsum, hspread = _head_norm_constants(Q_COLS)
    log2e = 1.0 / math.log(2.0)
    near, far = _rel_bias_tables(rel_bias * log2e)
    key_chunk = jnp.arange(FAR_MASK_CHUNKS * CHUNK) // CHUNK
    query_chunk = jnp.arange(ATT_ROWS) // CHUNK
    farmask = jnp.where(key_chunk[:, None] >= query_chunk[None, :], 0.0, -jnp.inf).astype(F32)
    qn = jnp.tile(q_norm, ATT_HEADS).reshape(1, ATT_DIM) * (log2e / math.sqrt(HEAD_DIM))
    return pl.pallas_call(
        _attn_layer_kernel,
        grid=(b, s // ATT_ROWS),
        in_specs=[
            pl.BlockSpec((1, ATT_ROWS, D_MODEL), lambda i, j: (i, j, 0)),
            _resident((1, D_MODEL)),
            _resident((D_MODEL, ATT_DIM)),
            _resident((1, ATT_DIM)),
            _resident(hsum.shape),
            _resident(hspread.shape),
            pl.BlockSpec((1, s + PAD_LEN, ATT_DIM), lambda i, j: (i, 0, 0), pipeline_mode=pl.Buffered(1)),
            pl.BlockSpec((1, ATT_DIM, s + PAD_LEN), lambda i, j: (i, 0, 0), pipeline_mode=pl.Buffered(1)),
            _resident(near.shape),
            _resident(far.shape),
            _resident(farmask.shape),
            _resident((ATT_DIM, D_MODEL)),
        ],
        out_specs=pl.BlockSpec((1, ATT_ROWS, D_MODEL), lambda i, j: (i, j, 0)),
        out_shape=jax.ShapeDtypeStruct((b, s, D_MODEL), F32),
        scratch_shapes=[
            pltpu.VMEM((ATT_ROWS, ATT_DIM), BF16),
            pltpu.VMEM((ATT_DIM, ATT_ROWS), BF16),
        ],
        compiler_params=_params("arbitrary", "arbitrary"),
        name="attn_layer",
    )(x, g.reshape(1, D_MODEL), w_q.astype(BF16), qn, hsum, hspread, k_pad, vt_pad,
      near, far, farmask, w_o.astype(BF16))


def kernel(x, ffn1_norm, ffn1_w_gate, ffn1_w_up, ffn1_w_down, ffn2_norm, ffn2_w_gate, ffn2_w_up,
           ffn2_w_down, ssm_norm, ssm_in_proj, ssm_conv_w, ssm_conv_b, ssm_dt_bias, ssm_A_log, ssm_D,
           ssm_out_norm, ssm_out_proj, kv_norm, w_kv, k_norm, att_norm, att_w_q, att_q_norm,
           att_rel_bias, att_w_o):
    b, s, d = x.shape
    depth = ffn1_norm.shape[0]
    n_ssm = ssm_norm.shape[0]
    t = b * s
    k_pad = vt_pad = None
    for l in range(depth):
        x = _ffn(x.reshape(t, d), ffn1_norm[l], ffn1_w_gate[l].astype(BF16),
                 ffn1_w_up[l].astype(BF16), ffn1_w_down[l].astype(BF16)).reshape(b, s, d)
        if l < n_ssm:
            x = _ssm(x, ssm_norm[l], ssm_in_proj[l], ssm_conv_w[l], ssm_conv_b[l], ssm_dt_bias[l],
                     ssm_A_log[l], ssm_D[l], ssm_out_norm[l], ssm_out_proj[l])
        else:
            if l == n_ssm:
                k_pad, vt_pad = _shared_kv(x, kv_norm, w_kv, k_norm)
            j = l - n_ssm
            x = _attn_layer(x, k_pad, vt_pad, att_norm[j], att_w_q[j], att_q_norm[j],
                            att_rel_bias[j], att_w_o[j])
        x = _ffn(x.reshape(t, d), ffn2_norm[l], ffn2_w_gate[l].astype(BF16),
                 ffn2_w_up[l].astype(BF16), ffn2_w_down[l].astype(BF16)).reshape(b, s, d)
    return x
```

```python
import functools
import math

import jax
import jax.numpy as jnp
from jax import lax
from jax.experimental import pallas as pl
from jax.experimental.pallas import tpu as pltpu

F32 = jnp.float32
BF16 = jnp.bfloat16

D_MODEL = 1024
D_FF = 2816
EPS = 1e-6

D_INNER = 2048
HEAD_DIM = 64
SSM_HEADS = 32
SSM_GROUPS = 8
D_STATE = 128
CONV_W = 4
CONV_DIM = D_INNER + 2 * SSM_GROUPS * D_STATE
GROUP_W = D_INNER // SSM_GROUPS
CHUNK = 64
CONV_COLS = 512

ATT_HEADS = 16
ATT_DIM = 1024
LEFT_CHUNKS = 8
BAND = (LEFT_CHUNKS + 1) * CHUNK
PAD_LEN = LEFT_CHUNKS * CHUNK
MAX_REL = 128

LANES = 128
VMEM_LIMIT = 56 * 1024 * 1024

FFN_ROWS = 512
SSM_ROWS = 256
Q_COLS = 256
SCORE_LOOKAHEAD = 2
ATT_ROWS = 256
ATT_WIN = ATT_ROWS + PAD_LEN
NEAR_ROWS = ATT_WIN - (PAD_LEN - MAX_REL)
FAR_MASK_CHUNKS = ATT_ROWS // CHUNK - 1
TOEP_W = 640


def _resident(shape):
    nd = len(shape)
    return pl.BlockSpec(shape, lambda *_: (0,) * nd, pipeline_mode=pl.Buffered(1))


def _resident_layer(stack_shape, layer):
    rest = tuple(stack_shape[1:])
    return pl.BlockSpec((None,) + rest, lambda *_: (layer,) + (0,) * len(rest),
                        pipeline_mode=pl.Buffered(1))


def _params(*sem):
    return pltpu.CompilerParams(dimension_semantics=sem, vmem_limit_bytes=VMEM_LIMIT)


def _rms(x, g):
    return x * lax.rsqrt(jnp.mean(x * x, axis=-1, keepdims=True) + EPS) * g


def _dot(a, b):
    return jnp.dot(a, b, preferred_element_type=F32)


def _dot_nt(a, b):
    return lax.dot_general(a, b, (((1,), (1,)), ((), ())), preferred_element_type=F32)


def _dot_tn(a, b):
    return lax.dot_general(a, b, (((0,), (0,)), ((), ())), preferred_element_type=F32)


def _split2(v):
    hi = v.astype(BF16)
    lo = (v - hi.astype(F32)).astype(BF16)
    return hi, lo


def _silu(v):
    return v * jax.nn.sigmoid(v)


def _ffn_kernel(x_ref, g_ref, wg_ref, wu_ref, wd_ref, o_ref):
    half = x_ref.shape[0] // 2
    hidden = []
    for r in range(2):
        x = x_ref[r * half:(r + 1) * half, :]
        xn = _rms(x, g_ref[...]).astype(BF16)
        hidden.append((_dot(xn, wg_ref[...]), _dot(xn, wu_ref[...])))
    for r in range(2):
        gate, up = hidden[r]
        h = (_silu(gate) * up).astype(BF16)
        rows = slice(r * half, (r + 1) * half)
        o_ref[rows, :] = x_ref[rows, :] + 0.5 * _dot(h, wd_ref[...])


def _ffn(x2, g, wg, wu, wd, layer):
    t = x2.shape[0]
    tm = min(FFN_ROWS, t)
    return pl.pallas_call(
        _ffn_kernel,
        grid=(t // tm,),
        in_specs=[
            pl.BlockSpec((tm, D_MODEL), lambda i: (i, 0)),
            _resident((1, D_MODEL)),
            _resident_layer(wg.shape, layer),
            _resident_layer(wu.shape, layer),
            _resident_layer(wd.shape, layer),
        ],
        out_specs=pl.BlockSpec((tm, D_MODEL), lambda i: (i, 0)),
        out_shape=jax.ShapeDtypeStruct((t, D_MODEL), F32),
        compiler_params=_params("arbitrary"),
        name="ffn",
    )(x2, g.reshape(1, D_MODEL), wg, wu, wd)


def _ssm_kernel(x_ref, g_ref, win_ref, wdt_ref, cw_ref, cb_ref, dtb_ref,
                alog_ref, alogf_ref, dskip_ref, onorm_ref, wout_ref,
                expand_ref, triu_ref, negmask_ref, tril_ref, o_ref,
                h_scr, tail_scr, xpad_scr, xs_scr, bc_scr, z_scr, dtf_scr, acsf_scr, y_scr):
    ts = x_ref.shape[1]

    @pl.when(pl.program_id(1) == 0)
    def _():
        h_scr[...] = jnp.zeros_like(h_scr)
        tail_scr[...] = jnp.zeros_like(tail_scr)

    x = x_ref[0]
    xn = _rms(x, g_ref[...]).astype(BF16)

    n_grp = CONV_DIM // CONV_COLS
    n_xs = D_INNER // CONV_COLS

    def project(g):
        return _dot(xn, win_ref[:, D_INNER + g * CONV_COLS:D_INNER + (g + 1) * CONV_COLS])

    def conv_group(g, pre):
        cols = slice(g * CONV_COLS, (g + 1) * CONV_COLS)
        slot = g % 2
        xpad_scr[slot, 0:8, :] = tail_scr[:, cols]
        xpad_scr[slot, 8:, :] = pre
        tail_scr[:, cols] = pre[ts - 8:, :]
        conv = cb_ref[:, cols] + cw_ref[CONV_W - 1:CONV_W, cols] * pre
        for k in range(CONV_W - 1):
            off = 8 - (CONV_W - 1) + k
            conv = conv + cw_ref[k:k + 1, cols] * xpad_scr[slot, off:off + ts, :]
        act = _silu(conv)
        if g < n_xs:
            xs_scr[:, cols] = act
        else:
            bc_scr[:, (g - n_xs) * CONV_COLS:(g - n_xs + 1) * CONV_COLS] = act.astype(BF16)

    a_head = -jnp.exp(alog_ref[...])
    a_full = -jnp.exp(alogf_ref[...])

    def expand(v):
        hi, lo = _split2(v)
        return _dot(jnp.concatenate([hi, lo], axis=1), expand_ref[...])

    pre_next = project(0)
    for g in range(n_grp):
        pre = pre_next
        if g + 1 < n_grp:
            pre_next = project(g + 1)
        if g % 2 == 1:
            zc = slice((g // 2) * CONV_COLS, (g // 2 + 1) * CONV_COLS)
            z_scr[:, zc] = _dot(xn, win_ref[:, zc])
        if g == 0:
            dt = jax.nn.softplus(_dot(xn, wdt_ref[...]) + dtb_ref[...])
        elif g == 2:
            dtf_scr[...] = expand(dt)
        elif g == 4:
            a = dt * a_head
            hi = a.astype(BF16)
            r1 = a - hi.astype(F32)
            mid = r1.astype(BF16)
            lo = (r1 - mid.astype(F32)).astype(BF16)
            acs = _dot(tril_ref[...], jnp.concatenate([hi, mid, lo], axis=0))
        elif g == 6:
            acsf_scr[...] = expand(acs)
        conv_group(g, pre)

    lane_head = lax.broadcasted_iota(jnp.int32, (1, GROUP_W), 1) // HEAD_DIM
    head_lanes = [(lane_head == r).astype(BF16) for r in range(GROUP_W // HEAD_DIM)]
    groups = range(SSM_GROUPS)
    gls = [slice(g * GROUP_W, (g + 1) * GROUP_W) for g in groups]

    def chunk_body(c, carry):
        r0 = pl.multiple_of(c * CHUNK, CHUNK)
        rows = pl.ds(r0, CHUNK)
        b_bf = [bc_scr[rows, g * D_STATE:(g + 1) * D_STATE] for g in groups]
        c_bf = [bc_scr[rows, (SSM_GROUPS + g) * D_STATE:(SSM_GROUPS + g + 1) * D_STATE] for g in groups]
        cb4 = [_dot_nt(c_bf[g], jnp.concatenate([b_bf[g]] * 4, axis=0)) for g in groups]

        y_off = [_dot(c_bf[g], h_scr[:, gls[g]].astype(BF16)) for g in groups]
        acs_last = acsf_scr[pl.ds(r0 + CHUNK - 1, 1), :]
        dec_chunk = jnp.exp(acs_last)
        for g in groups:
            gl = gls[g]
            acs_g = acsf_scr[rows, gl]
            xdec = (xs_scr[rows, gl] * dtf_scr[rows, gl] * jnp.exp(acs_last[:, gl] - acs_g)).astype(BF16)
            s_g = _dot_tn(b_bf[g], xdec)
            h_scr[:, gl] = h_scr[:, gl] * dec_chunk[:, gl] + s_g

        for g in groups:
            gl = gls[g]
            acs_g = acsf_scr[rows, gl]
            dt_g = dtf_scr[rows, gl]
            xs_g = xs_scr[rows, gl]
            acs_t = jnp.sum(dt_g * a_full[:, gl] * triu_ref[:, gl], axis=0, keepdims=True)
            lmat = jnp.exp(acs_g - acs_t + negmask_ref[:, gl])
            w = (cb4[g] * lmat).astype(BF16)
            x_g = (xs_g * dt_g).astype(BF16)
            bd = jnp.concatenate([x_g * m for m in head_lanes], axis=0)
            y_scr[rows, gl] = (_dot(w, bd) + y_off[g] * jnp.exp(acs_g) + dskip_ref[:, gl] * xs_g)
        return carry

    lax.fori_loop(0, ts // CHUNK, chunk_body, 0, unroll=True)

    out = x
    half = D_INNER // 2
    for hf in range(2):
        parts = []
        for g in range(hf * SSM_GROUPS // 2, (hf + 1) * SSM_GROUPS // 2):
            gg = y_scr[:, gls[g]] * _silu(z_scr[:, gls[g]])
            parts.append(gg * lax.rsqrt(jnp.mean(gg * gg, axis=-1, keepdims=True) + EPS))
        cols = slice(hf * half, (hf + 1) * half)
        yn = (jnp.concatenate(parts, axis=1) * onorm_ref[:, cols]).astype(BF16)
        out = out + _dot(yn, wout_ref[cols, :])
    o_ref[0] = out


def _ssm_constants(ts):
    h_of_lane = jnp.arange(D_INNER) // HEAD_DIM
    e0 = (jnp.arange(LANES)[:, None] == h_of_lane[None, :]).astype(BF16)
    expand = jnp.concatenate([e0, e0], axis=0)
    t_lane = jnp.arange(D_INNER) % HEAD_DIM
    t_row = jnp.arange(CHUNK)
    triu = (t_row[:, None] <= t_lane[None, :]).astype(F32)
    negmask = jnp.where(t_lane[None, :] <= t_row[:, None], 0.0, -jnp.inf).astype(F32)
    r = jnp.arange(ts)
    tril = ((r[None, :] <= r[:, None]) & (r[None, :] // CHUNK == r[:, None] // CHUNK)).astype(BF16)
    return expand, triu, negmask, jnp.concatenate([tril] * 3, axis=1)


def _ssm(x, g, w_in_stack, w_dt, conv_w, conv_b, dt_bias, a_log, d_skip, out_norm, w_out_stack, layer):
    b, s, _ = x.shape
    ts = min(SSM_ROWS, s)
    wdt = jnp.pad(w_dt, ((0, 0), (0, LANES - SSM_HEADS))).astype(BF16)
    pad_h = lambda v: jnp.pad(v.reshape(1, SSM_HEADS), ((0, 0), (0, LANES - SSM_HEADS)))
    rep = lambda v: jnp.repeat(v, HEAD_DIM).reshape(1, D_INNER)
    expand, triu, negmask, tril = _ssm_constants(ts)
    operands = (
        x, g.reshape(1, D_MODEL), w_in_stack, wdt, conv_w, conv_b.reshape(1, CONV_DIM),
        pad_h(dt_bias), pad_h(a_log), rep(a_log), rep(d_skip), out_norm.reshape(1, D_INNER),
        w_out_stack, expand, triu, negmask, tril)
    in_specs = [pl.BlockSpec((1, ts, D_MODEL), lambda i, j: (i, j, 0))]
    in_specs += [_resident_layer(op.shape, layer) if op is w_in_stack or op is w_out_stack
                 else _resident(op.shape) for op in operands[1:]]
    return pl.pallas_call(
        _ssm_kernel,
        grid=(b, s // ts),
        in_specs=in_specs,
        out_specs=pl.BlockSpec((1, ts, D_MODEL), lambda i, j: (i, j, 0)),
        out_shape=jax.ShapeDtypeStruct((b, s, D_MODEL), F32),
        scratch_shapes=[
            pltpu.VMEM((D_STATE, D_INNER), F32),
            pltpu.VMEM((8, CONV_DIM), F32),
            pltpu.VMEM((2, ts + 8, CONV_COLS), F32),
            pltpu.VMEM((ts, D_INNER), F32),
            pltpu.VMEM((ts, CONV_DIM - D_INNER), BF16),
            pltpu.VMEM((ts, D_INNER), F32),
            pltpu.VMEM((ts, D_INNER), F32),
            pltpu.VMEM((ts, D_INNER), F32),
            pltpu.VMEM((ts, D_INNER), F32),
        ],
        compiler_params=_params("arbitrary", "arbitrary"),
        name="ssm",
    )(*operands)


def _head_norm_constants(width=ATT_DIM):
    c = jnp.arange(width) // HEAD_DIM
    s0 = (c[:, None] == jnp.arange(LANES)[None, :]).astype(BF16)
    return jnp.concatenate([s0, s0], axis=0), jnp.concatenate([s0.T, s0.T], axis=0)


def _head_norm(v, sum_ref, spread_ref):
    hi, lo = _split2(v * v)
    ms = _dot(jnp.concatenate([hi, lo], axis=1), sum_ref[...]) * (1.0 / HEAD_DIM)
    rh, rl = _split2(lax.rsqrt(ms + EPS))
    return v * _dot(jnp.concatenate([rh, rl], axis=1), spread_ref[...])


def _kv_kernel(x_ref, g_ref, wk_ref, wvt_ref, kn_ref, sum_ref, spread_ref, k_ref, vt_ref):
    i = pl.program_id(1)

    @pl.when(i == 0)
    def _():
        k_ref[...] = jnp.zeros_like(k_ref)
        vt_ref[...] = jnp.zeros_like(vt_ref)

    @pl.when(i > 0)
    def _():
        xn = _rms(x_ref[0], g_ref[...]).astype(BF16)
        k = _head_norm(_dot(xn, wk_ref[...]), sum_ref, spread_ref) * kn_ref[...]
        k_ref[0] = k.astype(BF16)
        vt_ref[0] = _dot_nt(wvt_ref[...], xn).astype(BF16)


def _shared_kv(x, g, w_kv, k_norm):
    b, s, _ = x.shape
    tk = PAD_LEN
    hsum, hspread = _head_norm_constants()
    return pl.pallas_call(
        _kv_kernel,
        grid=(b, s // tk + 1),
        in_specs=[
            pl.BlockSpec((1, tk, D_MODEL), lambda i, j: (i, jnp.maximum(j - 1, 0), 0)),
            _resident((1, D_MODEL)),
            _resident((D_MODEL, ATT_DIM)),
            _resident((ATT_DIM, D_MODEL)),
            _resident((1, ATT_DIM)),
            _resident(hsum.shape),
            _resident(hspread.shape),
        ],
        out_specs=[pl.BlockSpec((1, tk, ATT_DIM), lambda i, j: (i, j, 0)),
                   pl.BlockSpec((1, ATT_DIM, tk), lambda i, j: (i, 0, j))],
        out_shape=[jax.ShapeDtypeStruct((b, s + PAD_LEN, ATT_DIM), BF16),
                   jax.ShapeDtypeStruct((b, ATT_DIM, s + PAD_LEN), BF16)],
        compiler_params=_params("arbitrary", "arbitrary"),
        name="shared_kv",
    )(x, g.reshape(1, D_MODEL), w_kv[:, :ATT_DIM].astype(BF16), w_kv[:, ATT_DIM:].T.astype(BF16),
      jnp.tile(k_norm, ATT_HEADS).reshape(1, ATT_DIM), hsum, hspread)


def _bias_kernel(w_ref, o_ref):
    rows = jnp.broadcast_to(w_ref[0], (NEAR_ROWS, TOEP_W))
    toep = pltpu.roll(rows, 0, 1, stride=1, stride_axis=0)[:, 0:ATT_ROWS]
    key_chunk = lax.broadcasted_iota(jnp.int32, (NEAR_ROWS, ATT_ROWS), 0) // CHUNK + (ATT_WIN - NEAR_ROWS) // CHUNK
    query_chunk = lax.broadcasted_iota(jnp.int32, (NEAR_ROWS, ATT_ROWS), 1) // CHUNK
    o_ref[0] = jnp.where(key_chunk - query_chunk <= LEFT_CHUNKS, toep, -jnp.inf)


def _rel_bias_tables(rel_bias):
    far = rel_bias[:, :1]
    w = jnp.concatenate([
        jnp.broadcast_to(far, (ATT_HEADS, ATT_ROWS + 1)),
        jnp.broadcast_to(rel_bias[:, -1:], (ATT_HEADS, TOEP_W - ATT_ROWS - 1 - 2 * MAX_REL)),
        rel_bias[:, :0:-1],
    ], axis=1).reshape(ATT_HEADS, 1, TOEP_W)
    near = pl.pallas_call(
        _bias_kernel,
        grid=(ATT_HEADS,),
        in_specs=[pl.BlockSpec((1, 1, TOEP_W), lambda h: (h, 0, 0))],
        out_specs=pl.BlockSpec((1, NEAR_ROWS, ATT_ROWS), lambda h: (h, 0, 0)),
        out_shape=jax.ShapeDtypeStruct((ATT_HEADS, NEAR_ROWS, ATT_ROWS), F32),
        compiler_params=_params("arbitrary"),
        name="rel_bias",
    )(w)
    return near, jnp.broadcast_to(far[:, :, None], (ATT_HEADS, 1, ATT_ROWS))


def _attn_layer_kernel(x_ref, g_ref, wq_ref, qn_ref, sum_ref, spread_ref, k_ref, vt_ref,
                       near_ref, far_ref, farmask_ref, wo_ref, o_ref, q_scr, ot_scr):
    base = pl.multiple_of(pl.program_id(1) * ATT_ROWS, ATT_ROWS)
    x = x_ref[0]
    xn = _rms(x, g_ref[...]).astype(BF16)

    def q_project(g):
        return _dot(xn, wq_ref[:, g * Q_COLS:(g + 1) * Q_COLS])

    def q_mean_sq(qraw):
        hi, lo = _split2(qraw * qraw)
        return _dot(jnp.concatenate([hi, lo], axis=1), sum_ref[...]) * (1.0 / HEAD_DIM)

    def q_inv_rms(ms):
        rh, rl = _split2(lax.rsqrt(ms + EPS))
        return _dot(jnp.concatenate([rh, rl], axis=1), spread_ref[...])

    def q_store(g, qraw, inv):
        cols = slice(g * Q_COLS, (g + 1) * Q_COLS)
        q_scr[:, cols] = (qraw * inv * qn_ref[:, cols]).astype(BF16)

    qraw = q_project(0)
    q_store(0, qraw, q_inv_rms(q_mean_sq(qraw)))

    lane = lax.broadcasted_iota(jnp.int32, (1, LANES), 1)
    head_mask = [(lane < HEAD_DIM).astype(BF16), (lane >= HEAD_DIM).astype(BF16)]
    n_kc = ATT_WIN // CHUNK
    far_kc = (ATT_WIN - NEAR_ROWS) // CHUNK
    pad_neg = [jnp.where(base + kc * CHUNK < PAD_LEN, -jnp.inf, 0.0).astype(F32)
               for kc in range(PAD_LEN // CHUNK)]

    chunks_per_half = LANES // CHUNK
    seen_kc = LEFT_CHUNKS + chunks_per_half

    def scores_t(h):
        l0 = (h // 2) * LANES
        q_pair = q_scr[:, pl.ds(l0, LANES)]
        k_win = k_ref[0, pl.ds(base, ATT_WIN), pl.ds(l0, LANES)]
        st = _dot_nt(k_win, q_pair * head_mask[h % 2])
        far_row = far_ref[h]
        halves = []
        for hf in range(ATT_ROWS // LANES):
            lanes = slice(hf * LANES, (hf + 1) * LANES)
            first_kc = hf * chunks_per_half
            blocks = []
            for kc in range(first_kc, first_kc + seen_kc):
                rows = slice(kc * CHUNK, (kc + 1) * CHUNK)
                blk = st[rows, lanes]
                if kc < far_kc:
                    blk = blk + (far_row[:, lanes] + pad_neg[kc])
                    if kc == first_kc:
                        blk = blk + farmask_ref[rows, lanes]
                else:
                    blk = blk + near_ref[h, (kc - far_kc) * CHUNK:(kc - far_kc + 1) * CHUNK, lanes]
                    if kc < len(pad_neg):
                        blk = blk + pad_neg[kc]
                blocks.append(blk)
            s = jnp.concatenate(blocks, axis=0)
            halves.append((s, jnp.max(s, axis=0, keepdims=True)))
        return halves

    heads_per_group = Q_COLS // HEAD_DIM
    half_dim = ATT_DIM // 2
    out = x
    qraw = q_project(1)
    pending = [scores_t(h) for h in range(SCORE_LOOKAHEAD)]
    for h in range(ATT_HEADS):
        halves = pending.pop(0)
        g_next, stage = divmod(h + SCORE_LOOKAHEAD + heads_per_group - 1, heads_per_group)
        if 1 <= g_next < ATT_DIM // Q_COLS:
            if stage == 0:
                qraw = q_project(g_next)
            elif stage == 1:
                ms = q_mean_sq(qraw)
            elif stage == 2:
                inv = q_inv_rms(ms)
            else:
                q_store(g_next, qraw, inv)
        if h + SCORE_LOOKAHEAD < ATT_HEADS:
            pending.append(scores_t(h + SCORE_LOOKAHEAD))
        p_cols, denoms = [], []
        for hf, (s, s_max) in enumerate(halves):
            p_half = jnp.exp2(s - s_max)
            denoms.append(jnp.sum(p_half, axis=0, keepdims=True))
            above = jnp.zeros((hf * chunks_per_half * CHUNK, LANES), BF16)
            below = jnp.zeros((ATT_WIN - (hf * chunks_per_half + seen_kc) * CHUNK, LANES), BF16)
            p_cols.append(jnp.concatenate(
                [blk for blk in (above, p_half.astype(BF16), below) if blk.shape[0]], axis=0))
        p = jnp.concatenate(p_cols, axis=1)
        denom = jnp.concatenate(denoms, axis=1)
        r0 = h * HEAD_DIM
        vt = vt_ref[0, pl.ds(r0, HEAD_DIM), pl.ds(base, ATT_WIN)]
        ot = _dot(vt, p) / denom
        ot_scr[pl.ds(r0, HEAD_DIM), :] = ot.astype(BF16)
        if h == ATT_HEADS // 2:
            out = out + _dot_tn(ot_scr[0:half_dim, :], wo_ref[0:half_dim, :])
    o_ref[0] = out + _dot_tn(ot_scr[half_dim:, :], wo_ref[half_dim:, :])


def _attn_layer(x, k_pad, vt_pad, g, w_q, q_norm, rel_bias, w_o):
    b, s, _ = x.shape
    hsum, hspread = _head_norm_constants(Q_COLS)
    log2e = 1.0 / math.log(2.0)
    near, far = _rel_bias_tables(rel_bias * log2e)
    key_chunk = jnp.arange(FAR_MASK_CHUNKS * CHUNK) // CHUNK
    query_chunk = jnp.arange(ATT_ROWS) // CHUNK
    farmask = jnp.where(key_chunk[:, None] >= query_chunk[None, :], 0.0, -jnp.inf).astype(F32)
    qn = jnp.tile(q_norm, ATT_HEADS).reshape(1, ATT_DIM) * (log2e / math.sqrt(HEAD_DIM))
    return pl.pallas_call(
        _attn_layer_kernel,
        grid=(b, s // ATT_ROWS),
        in_specs=[
            pl.BlockSpec((1, ATT_ROWS, D_MODEL), lambda i, j: (i, j, 0)),
            _resident((1, D_MODEL)),
            _resident((D_MODEL, ATT_DIM)),
            _resident((1, ATT_DIM)),
            _resident(hsum.shape),
            _resident(hspread.shape),
            pl.BlockSpec((1, s + PAD_LEN, ATT_DIM), lambda i, j: (i, 0, 0)),
            pl.BlockSpec((1, ATT_DIM, s + PAD_LEN), lambda i, j: (i, 0, 0)),
            _resident(near.shape),
            _resident(far.shape),
            _resident(farmask.shape),
            _resident((ATT_DIM, D_MODEL)),
        ],
        out_specs=pl.BlockSpec((1, ATT_ROWS, D_MODEL), lambda i, j: (i, j, 0)),
        out_shape=jax.ShapeDtypeStruct((b, s, D_MODEL), F32),
        scratch_shapes=[
            pltpu.VMEM((ATT_ROWS, ATT_DIM), BF16),
            pltpu.VMEM((ATT_DIM, ATT_ROWS), BF16),
        ],
        compiler_params=_params("arbitrary", "arbitrary"),
        name="attn_layer",
    )(x, g.reshape(1, D_MODEL), w_q.astype(BF16), qn, hsum, hspread, k_pad, vt_pad,
      near, far, farmask, w_o.astype(BF16))


def kernel(x, ffn1_norm, ffn1_w_gate, ffn1_w_up, ffn1_w_down, ffn2_norm, ffn2_w_gate, ffn2_w_up,
           ffn2_w_down, ssm_norm, ssm_in_proj, ssm_conv_w, ssm_conv_b, ssm_dt_bias, ssm_A_log, ssm_D,
           ssm_out_norm, ssm_out_proj, kv_norm, w_kv, k_norm, att_norm, att_w_q, att_q_norm,
           att_rel_bias, att_w_o):
    b, s, d = x.shape
    depth = ffn1_norm.shape[0]
    n_ssm = ssm_norm.shape[0]
    t = b * s
    k_pad = vt_pad = None
    ffn1 = [w.astype(BF16) for w in (ffn1_w_gate, ffn1_w_up, ffn1_w_down)]
    ffn2 = [w.astype(BF16) for w in (ffn2_w_gate, ffn2_w_up, ffn2_w_down)]
    ssm_in = ssm_in_proj.astype(BF16)
    ssm_out = ssm_out_proj.astype(BF16)
    for l in range(depth):
        x = _ffn(x.reshape(t, d), ffn1_norm[l], *ffn1, l).reshape(b, s, d)
        if l < n_ssm:
            x = _ssm(x, ssm_norm[l], ssm_in, ssm_in_proj[l, :, D_INNER + CONV_DIM:], ssm_conv_w[l],
                     ssm_conv_b[l], ssm_dt_bias[l], ssm_A_log[l], ssm_D[l], ssm_out_norm[l], ssm_out, l)
        else:
            if l == n_ssm:
                k_pad, vt_pad = _shared_kv(x, kv_norm, w_kv, k_norm)
            j = l - n_ssm
            x = _attn_layer(x, k_pad, vt_pad, att_norm[j], att_w_q[j], att_q_norm[j],
                            att_rel_bias[j], att_w_o[j])
        x = _ffn(x.reshape(t, d), ffn2_norm[l], *ffn2, l).reshape(b, s, d)
    return x
```

```python
import functools
import math

import jax
import jax.numpy as jnp
from jax import lax
from jax.experimental import pallas as pl
from jax.experimental.pallas import tpu as pltpu

F32 = jnp.float32
BF16 = jnp.bfloat16

D_MODEL = 1024
D_FF = 2816
EPS = 1e-6

D_INNER = 2048
HEAD_DIM = 64
SSM_HEADS = 32
SSM_GROUPS = 8
D_STATE = 128
CONV_W = 4
CONV_DIM = D_INNER + 2 * SSM_GROUPS * D_STATE
GROUP_W = D_INNER // SSM_GROUPS
CHUNK = 64
CONV_COLS = 256

ATT_HEADS = 16
ATT_DIM = 1024
LEFT_CHUNKS = 8
BAND = (LEFT_CHUNKS + 1) * CHUNK
PAD_LEN = LEFT_CHUNKS * CHUNK
MAX_REL = 128

LANES = 128
VMEM_LIMIT = 56 * 1024 * 1024

FFN_ROWS = 512
SSM_ROWS = 256
Q_COLS = 256
SCORE_LOOKAHEAD = 2
ATT_ROWS = 256
ATT_WIN = ATT_ROWS + PAD_LEN
NEAR_ROWS = ATT_WIN - (PAD_LEN - MAX_REL)
FAR_MASK_CHUNKS = ATT_ROWS // CHUNK - 1
TOEP_W = 640


def _resident(shape):
    nd = len(shape)
    return pl.BlockSpec(shape, lambda *_: (0,) * nd, pipeline_mode=pl.Buffered(1))


def _resident_layer(stack_shape, layer):
    rest = tuple(stack_shape[1:])
    return pl.BlockSpec((None,) + rest, lambda *_: (layer,) + (0,) * len(rest),
                        pipeline_mode=pl.Buffered(1))


def _params(*sem):
    return pltpu.CompilerParams(dimension_semantics=sem, vmem_limit_bytes=VMEM_LIMIT)


def _rms(x, g):
    return x * lax.rsqrt(jnp.mean(x * x, axis=-1, keepdims=True) + EPS) * g


def _dot(a, b):
    return jnp.dot(a, b, preferred_element_type=F32)


def _dot_nt(a, b):
    return lax.dot_general(a, b, (((1,), (1,)), ((), ())), preferred_element_type=F32)


def _dot_tn(a, b):
    return lax.dot_general(a, b, (((0,), (0,)), ((), ())), preferred_element_type=F32)


def _split2(v):
    hi = v.astype(BF16)
    lo = (v - hi.astype(F32)).astype(BF16)
    return hi, lo


def _silu(v):
    return v * jax.nn.sigmoid(v)


def _ffn_kernel(x_ref, g_ref, wg_ref, wu_ref, wd_ref, o_ref):
    half = x_ref.shape[0] // 2
    hidden = []
    for r in range(2):
        x = x_ref[r * half:(r + 1) * half, :]
        xn = _rms(x, g_ref[...]).astype(BF16)
        hidden.append((_dot(xn, wg_ref[...]), _dot(xn, wu_ref[...])))
    for r in range(2):
        gate, up = hidden[r]
        h = (_silu(gate) * up).astype(BF16)
        rows = slice(r * half, (r + 1) * half)
        o_ref[rows, :] = x_ref[rows, :] + 0.5 * _dot(h, wd_ref[...])


def _ffn(x2, g, wg, wu, wd, layer):
    t = x2.shape[0]
    tm = min(FFN_ROWS, t)
    return pl.pallas_call(
        _ffn_kernel,
        grid=(t // tm,),
        in_specs=[
            pl.BlockSpec((tm, D_MODEL), lambda i: (i, 0)),
            _resident((1, D_MODEL)),
            _resident_layer(wg.shape, layer),
            _resident_layer(wu.shape, layer),
            _resident_layer(wd.shape, layer),
        ],
        out_specs=pl.BlockSpec((tm, D_MODEL), lambda i: (i, 0)),
        out_shape=jax.ShapeDtypeStruct((t, D_MODEL), F32),
        compiler_params=_params("arbitrary"),
        name="ffn",
    )(x2, g.reshape(1, D_MODEL), wg, wu, wd)


def _ssm_kernel(x_ref, g_ref, win_ref, wdt_ref, cw_ref, cb_ref, dtb_ref,
                alog_ref, alogf_ref, dskip_ref, onorm_ref, wout_ref,
                expand_ref, triu_ref, negmask_ref, tril_ref, o_ref,
                h_scr, tail_scr, xpad_scr, xs_scr, bc_scr, z_scr, dtf_scr, acsf_scr, y_scr):
    ts = x_ref.shape[1]

    @pl.when(pl.program_id(1) == 0)
    def _():
        h_scr[...] = jnp.zeros_like(h_scr)
        tail_scr[...] = jnp.zeros_like(tail_scr)

    x = x_ref[0]
    xn = _rms(x, g_ref[...]).astype(BF16)

    n_grp = CONV_DIM // CONV_COLS
    n_xs = D_INNER // CONV_COLS

    def project(g):
        return _dot(xn, win_ref[:, D_INNER + g * CONV_COLS:D_INNER + (g + 1) * CONV_COLS])

    def conv_group(g, pre):
        cols = slice(g * CONV_COLS, (g + 1) * CONV_COLS)
        slot = g % 2
        xpad_scr[slot, 0:8, :] = tail_scr[:, cols]
        xpad_scr[slot, 8:, :] = pre
        tail_scr[:, cols] = pre[ts - 8:, :]
        conv = cb_ref[:, cols] + cw_ref[CONV_W - 1:CONV_W, cols] * pre
        for k in range(CONV_W - 1):
            off = 8 - (CONV_W - 1) + k
            conv = conv + cw_ref[k:k + 1, cols] * xpad_scr[slot, off:off + ts, :]
        act = _silu(conv)
        if g < n_xs:
            xs_scr[:, cols] = act
        else:
            bc_scr[:, (g - n_xs) * CONV_COLS:(g - n_xs + 1) * CONV_COLS] = act.astype(BF16)

    a_head = -jnp.exp(alog_ref[...])
    a_full = -jnp.exp(alogf_ref[...])

    def expand(v):
        hi, lo = _split2(v)
        return _dot(jnp.concatenate([hi, lo], axis=1), expand_ref[...])

    pre_next = project(0)
    for g in range(n_grp):
        pre = pre_next
        if g + 1 < n_grp:
            pre_next = project(g + 1)
        if g % 2 == 1:
            zc = slice((g // 2) * CONV_COLS, (g // 2 + 1) * CONV_COLS)
            z_scr[:, zc] = _dot(xn, win_ref[:, zc])
        if g == 0:
            dt = jax.nn.softplus(_dot(xn, wdt_ref[...]) + dtb_ref[...])
        elif g == n_grp // 4:
            dtf_scr[...] = expand(dt)
        elif g == n_grp // 2:
            a = dt * a_head
            hi = a.astype(BF16)
            r1 = a - hi.astype(F32)
            mid = r1.astype(BF16)
            lo = (r1 - mid.astype(F32)).astype(BF16)
            acs = _dot(tril_ref[...], jnp.concatenate([hi, mid, lo], axis=0))
        elif g == 3 * n_grp // 4:
            acsf_scr[...] = expand(acs)
        conv_group(g, pre)

    lane_head = lax.broadcasted_iota(jnp.int32, (1, GROUP_W), 1) // HEAD_DIM
    head_lanes = [(lane_head == r).astype(BF16) for r in range(GROUP_W // HEAD_DIM)]
    groups = range(SSM_GROUPS)
    gls = [slice(g * GROUP_W, (g + 1) * GROUP_W) for g in groups]

    def chunk_body(c, carry):
        r0 = pl.multiple_of(c * CHUNK, CHUNK)
        rows = pl.ds(r0, CHUNK)
        b_bf = [bc_scr[rows, g * D_STATE:(g + 1) * D_STATE] for g in groups]
        c_bf = [bc_scr[rows, (SSM_GROUPS + g) * D_STATE:(SSM_GROUPS + g + 1) * D_STATE] for g in groups]
        cb4 = [_dot_nt(c_bf[g], jnp.concatenate([b_bf[g]] * 4, axis=0)) for g in groups]

        y_off = [_dot(c_bf[g], h_scr[:, gls[g]].astype(BF16)) for g in groups]
        acs_last = acsf_scr[pl.ds(r0 + CHUNK - 1, 1), :]
        dec_chunk = jnp.exp(acs_last)
        for g in groups:
            gl = gls[g]
            acs_g = acsf_scr[rows, gl]
            xdec = (xs_scr[rows, gl] * dtf_scr[rows, gl] * jnp.exp(acs_last[:, gl] - acs_g)).astype(BF16)
            s_g = _dot_tn(b_bf[g], xdec)
            h_scr[:, gl] = h_scr[:, gl] * dec_chunk[:, gl] + s_g

        for g in groups:
            gl = gls[g]
            acs_g = acsf_scr[rows, gl]
            dt_g = dtf_scr[rows, gl]
            xs_g = xs_scr[rows, gl]
            acs_t = jnp.sum(dt_g * a_full[:, gl] * triu_ref[:, gl], axis=0, keepdims=True)
            lmat = jnp.exp(acs_g - acs_t + negmask_ref[:, gl])
            w = (cb4[g] * lmat).astype(BF16)
            x_g = (xs_g * dt_g).astype(BF16)
            bd = jnp.concatenate([x_g * m for m in head_lanes], axis=0)
            y_scr[rows, gl] = (_dot(w, bd) + y_off[g] * jnp.exp(acs_g) + dskip_ref[:, gl] * xs_g)
        return carry

    lax.fori_loop(0, ts // CHUNK, chunk_body, 0, unroll=True)

    out = x
    half = D_INNER // 2
    for hf in range(2):
        parts = []
        for g in range(hf * SSM_GROUPS // 2, (hf + 1) * SSM_GROUPS // 2):
            gg = y_scr[:, gls[g]] * _silu(z_scr[:, gls[g]])
            parts.append(gg * lax.rsqrt(jnp.mean(gg * gg, axis=-1, keepdims=True) + EPS))
        cols = slice(hf * half, (hf + 1) * half)
        yn = (jnp.concatenate(parts, axis=1) * onorm_ref[:, cols]).astype(BF16)
        out = out + _dot(yn, wout_ref[cols, :])
    o_ref[0] = out


def _ssm_constants(ts):
    h_of_lane = jnp.arange(D_INNER) // HEAD_DIM
    e0 = (jnp.arange(LANES)[:, None] == h_of_lane[None, :]).astype(BF16)
    expand = jnp.concatenate([e0, e0], axis=0)
    t_lane = jnp.arange(D_INNER) % HEAD_DIM
    t_row = jnp.arange(CHUNK)
    triu = (t_row[:, None] <= t_lane[None, :]).astype(F32)
    negmask = jnp.where(t_lane[None, :] <= t_row[:, None], 0.0, -jnp.inf).astype(F32)
    r = jnp.arange(ts)
    tril = ((r[None, :] <= r[:, None]) & (r[None, :] // CHUNK == r[:, None] // CHUNK)).astype(BF16)
    return expand, triu, negmask, jnp.concatenate([tril] * 3, axis=1)


def _ssm(x, g, w_in_stack, w_dt, conv_w, conv_b, dt_bias, a_log, d_skip, out_norm, w_out_stack, layer):
    b, s, _ = x.shape
    ts = min(SSM_ROWS, s)
    wdt = jnp.pad(w_dt, ((0, 0), (0, LANES - SSM_HEADS))).astype(BF16)
    pad_h = lambda v: jnp.pad(v.reshape(1, SSM_HEADS), ((0, 0), (0, LANES - SSM_HEADS)))
    rep = lambda v: jnp.repeat(v, HEAD_DIM).reshape(1, D_INNER)
    expand, triu, negmask, tril = _ssm_constants(ts)
    operands = (
        x, g.reshape(1, D_MODEL), w_in_stack, wdt, conv_w, conv_b.reshape(1, CONV_DIM),
        pad_h(dt_bias), pad_h(a_log), rep(a_log), rep(d_skip), out_norm.reshape(1, D_INNER),
        w_out_stack, expand, triu, negmask, tril)
    in_specs = [pl.BlockSpec((1, ts, D_MODEL), lambda i, j: (i, j, 0))]
    in_specs += [_resident_layer(op.shape, layer) if op is w_in_stack or op is w_out_stack
                 else _resident(op.shape) for op in operands[1:]]
    return pl.pallas_call(
        _ssm_kernel,
        grid=(b, s // ts),
        in_specs=in_specs,
        out_specs=pl.BlockSpec((1, ts, D_MODEL), lambda i, j: (i, j, 0)),
        out_shape=jax.ShapeDtypeStruct((b, s, D_MODEL), F32),
        scratch_shapes=[
            pltpu.VMEM((D_STATE, D_INNER), F32),
            pltpu.VMEM((8, CONV_DIM), F32),
            pltpu.VMEM((2, ts + 8, CONV_COLS), F32),
            pltpu.VMEM((ts, D_INNER), F32),
            pltpu.VMEM((ts, CONV_DIM - D_INNER), BF16),
            pltpu.VMEM((ts, D_INNER), F32),
            pltpu.VMEM((ts, D_INNER), F32),
            pltpu.VMEM((ts, D_INNER), F32),
            pltpu.VMEM((ts, D_INNER), F32),
        ],
        compiler_params=_params("arbitrary", "arbitrary"),
        name="ssm",
    )(*operands)


def _head_norm_constants(width=ATT_DIM):
    c = jnp.arange(width) // HEAD_DIM
    s0 = (c[:, None] == jnp.arange(LANES)[None, :]).astype(BF16)
    return jnp.concatenate([s0, s0], axis=0), jnp.concatenate([s0.T, s0.T], axis=0)


def _head_norm(v, sum_ref, spread_ref):
    hi, lo = _split2(v * v)
    ms = _dot(jnp.concatenate([hi, lo], axis=1), sum_ref[...]) * (1.0 / HEAD_DIM)
    rh, rl = _split2(lax.rsqrt(ms + EPS))
    return v * _dot(jnp.concatenate([rh, rl], axis=1), spread_ref[...])


def _kv_kernel(x_ref, g_ref, wk_ref, wvt_ref, kn_ref, sum_ref, spread_ref, k_ref, vt_ref):
    i = pl.program_id(1)

    @pl.when(i == 0)
    def _():
        k_ref[...] = jnp.zeros_like(k_ref)
        vt_ref[...] = jnp.zeros_like(vt_ref)

    @pl.when(i > 0)
    def _():
        xn = _rms(x_ref[0], g_ref[...]).astype(BF16)
        k = _head_norm(_dot(xn, wk_ref[...]), sum_ref, spread_ref) * kn_ref[...]
        k_ref[0] = k.astype(BF16)
        vt_ref[0] = _dot_nt(wvt_ref[...], xn).astype(BF16)


def _shared_kv(x, g, w_kv, k_norm):
    b, s, _ = x.shape
    tk = PAD_LEN
    hsum, hspread = _head_norm_constants()
    return pl.pallas_call(
        _kv_kernel,
        grid=(b, s // tk + 1),
        in_specs=[
            pl.BlockSpec((1, tk, D_MODEL), lambda i, j: (i, jnp.maximum(j - 1, 0), 0)),
            _resident((1, D_MODEL)),
            _resident((D_MODEL, ATT_DIM)),
            _resident((ATT_DIM, D_MODEL)),
            _resident((1, ATT_DIM)),
            _resident(hsum.shape),
            _resident(hspread.shape),
        ],
        out_specs=[pl.BlockSpec((1, tk, ATT_DIM), lambda i, j: (i, j, 0)),
                   pl.BlockSpec((1, ATT_DIM, tk), lambda i, j: (i, 0, j))],
        out_shape=[jax.ShapeDtypeStruct((b, s + PAD_LEN, ATT_DIM), BF16),
                   jax.ShapeDtypeStruct((b, ATT_DIM, s + PAD_LEN), BF16)],
        compiler_params=_params("arbitrary", "arbitrary"),
        name="shared_kv",
    )(x, g.reshape(1, D_MODEL), w_kv[:, :ATT_DIM].astype(BF16), w_kv[:, ATT_DIM:].T.astype(BF16),
      jnp.tile(k_norm, ATT_HEADS).reshape(1, ATT_DIM), hsum, hspread)


def _bias_kernel(w_ref, o_ref):
    rows = jnp.broadcast_to(w_ref[0], (NEAR_ROWS, TOEP_W))
    toep = pltpu.roll(rows, 0, 1, stride=1, stride_axis=0)[:, 0:ATT_ROWS]
    key_chunk = lax.broadcasted_iota(jnp.int32, (NEAR_ROWS, ATT_ROWS), 0) // CHUNK + (ATT_WIN - NEAR_ROWS) // CHUNK
    query_chunk = lax.broadcasted_iota(jnp.int32, (NEAR_ROWS, ATT_ROWS), 1) // CHUNK
    o_ref[0] = jnp.where(key_chunk - query_chunk <= LEFT_CHUNKS, toep, -jnp.inf)


def _rel_bias_tables(rel_bias):
    far = rel_bias[:, :1]
    w = jnp.concatenate([
        jnp.broadcast_to(far, (ATT_HEADS, ATT_ROWS + 1)),
        jnp.broadcast_to(rel_bias[:, -1:], (ATT_HEADS, TOEP_W - ATT_ROWS - 1 - 2 * MAX_REL)),
        rel_bias[:, :0:-1],
    ], axis=1).reshape(ATT_HEADS, 1, TOEP_W)
    near = pl.pallas_call(
        _bias_kernel,
        grid=(ATT_HEADS,),
        in_specs=[pl.BlockSpec((1, 1, TOEP_W), lambda h: (h, 0, 0))],
        out_specs=pl.BlockSpec((1, NEAR_ROWS, ATT_ROWS), lambda h: (h, 0, 0)),
        out_shape=jax.ShapeDtypeStruct((ATT_HEADS, NEAR_ROWS, ATT_ROWS), F32),
        compiler_params=_params("arbitrary"),
        name="rel_bias",
    )(w)
    return near, jnp.broadcast_to(far[:, :, None], (ATT_HEADS, 1, ATT_ROWS))


def _attn_layer_kernel(x_ref, g_ref, wq_ref, qn_ref, sum_ref, spread_ref, k_ref, vt_ref,
                       near_ref, far_ref, farmask_ref, wo_ref, o_ref, q_scr, ot_scr):
    base = pl.multiple_of(pl.program_id(1) * ATT_ROWS, ATT_ROWS)
    x = x_ref[0]
    xn = _rms(x, g_ref[...]).astype(BF16)

    def q_project(g):
        return _dot(xn, wq_ref[:, g * Q_COLS:(g + 1) * Q_COLS])

    def q_mean_sq(qraw):
        hi, lo = _split2(qraw * qraw)
        return _dot(jnp.concatenate([hi, lo], axis=1), sum_ref[...]) * (1.0 / HEAD_DIM)

    def q_inv_rms(ms):
        rh, rl = _split2(lax.rsqrt(ms + EPS))
        return _dot(jnp.concatenate([rh, rl], axis=1), spread_ref[...])

    def q_store(g, qraw, inv):
        cols = slice(g * Q_COLS, (g + 1) * Q_COLS)
        q_scr[:, cols] = (qraw * inv * qn_ref[:, cols]).astype(BF16)

    qraw = q_project(0)
    q_store(0, qraw, q_inv_rms(q_mean_sq(qraw)))

    lane = lax.broadcasted_iota(jnp.int32, (1, LANES), 1)
    head_mask = [(lane < HEAD_DIM).astype(BF16), (lane >= HEAD_DIM).astype(BF16)]
    n_kc = ATT_WIN // CHUNK
    far_kc = (ATT_WIN - NEAR_ROWS) // CHUNK
    pad_neg = [jnp.where(base + kc * CHUNK < PAD_LEN, -jnp.inf, 0.0).astype(F32)
               for kc in range(PAD_LEN // CHUNK)]

    chunks_per_half = LANES // CHUNK
    seen_kc = LEFT_CHUNKS + chunks_per_half

    def scores_t(h):
        l0 = (h // 2) * LANES
        q_pair = q_scr[:, pl.ds(l0, LANES)]
        k_win = k_ref[0, pl.ds(base, ATT_WIN), pl.ds(l0, LANES)]
        st = _dot_nt(k_win, q_pair * head_mask[h % 2])
        far_row = far_ref[h]
        halves = []
        for hf in range(ATT_ROWS // LANES):
            lanes = slice(hf * LANES, (hf + 1) * LANES)
            first_kc = hf * chunks_per_half
            blocks = []
            for kc in range(first_kc, first_kc + seen_kc):
                rows = slice(kc * CHUNK, (kc + 1) * CHUNK)
                blk = st[rows, lanes]
                if kc < far_kc:
                    blk = blk + (far_row[:, lanes] + pad_neg[kc])
                    if kc == first_kc:
                        blk = blk + farmask_ref[rows, lanes]
                else:
                    blk = blk + near_ref[h, (kc - far_kc) * CHUNK:(kc - far_kc + 1) * CHUNK, lanes]
                    if kc < len(pad_neg):
                        blk = blk + pad_neg[kc]
                blocks.append(blk)
            s = jnp.concatenate(blocks, axis=0)
            halves.append((s, jnp.max(s, axis=0, keepdims=True)))
        return halves

    heads_per_group = Q_COLS // HEAD_DIM
    half_dim = ATT_DIM // 2
    out = x
    qraw = q_project(1)
    pending = [scores_t(h) for h in range(SCORE_LOOKAHEAD)]
    for h in range(ATT_HEADS):
        halves = pending.pop(0)
        g_next, stage = divmod(h + SCORE_LOOKAHEAD + heads_per_group - 1, heads_per_group)
        if 1 <= g_next < ATT_DIM // Q_COLS:
            if stage == 0:
                qraw = q_project(g_next)
            elif stage == 1:
                ms = q_mean_sq(qraw)
            elif stage == 2:
                inv = q_inv_rms(ms)
            else:
                q_store(g_next, qraw, inv)
        if h + SCORE_LOOKAHEAD < ATT_HEADS:
            pending.append(scores_t(h + SCORE_LOOKAHEAD))
        p_cols, denoms = [], []
        for hf, (s, s_max) in enumerate(halves):
            p_half = jnp.exp2(s - s_max)
            denoms.append(jnp.sum(p_half, axis=0, keepdims=True))
            above = jnp.zeros((hf * chunks_per_half * CHUNK, LANES), BF16)
            below = jnp.zeros((ATT_WIN - (hf * chunks_per_half + seen_kc) * CHUNK, LANES), BF16)
            p_cols.append(jnp.concatenate(
                [blk for blk in (above, p_half.astype(BF16), below) if blk.shape[0]], axis=0))
        p = jnp.concatenate(p_cols, axis=1)
        denom = jnp.concatenate(denoms, axis=1)
        r0 = h * HEAD_DIM
        vt = vt_ref[0, pl.ds(r0, HEAD_DIM), pl.ds(base, ATT_WIN)]
        ot = _dot(vt, p) / denom
        ot_scr[pl.ds(r0, HEAD_DIM), :] = ot.astype(BF16)
        if h == ATT_HEADS // 2:
            out = out + _dot_tn(ot_scr[0:half_dim, :], wo_ref[0:half_dim, :])
    o_ref[0] = out + _dot_tn(ot_scr[half_dim:, :], wo_ref[half_dim:, :])


def _attn_layer(x, k_pad, vt_pad, g, w_q, q_norm, rel_bias, w_o):
    b, s, _ = x.shape
    hsum, hspread = _head_norm_constants(Q_COLS)
    log2e = 1.0 / math.log(2.0)
    near, far = _rel_bias_tables(rel_bias * log2e)
    key_chunk = jnp.arange(FAR_MASK_CHUNKS * CHUNK) // CHUNK
    query_chunk = jnp.arange(ATT_ROWS) // CHUNK
    farmask = jnp.where(key_chunk[:, None] >= query_chunk[None, :], 0.0, -jnp.inf).astype(F32)
    qn = jnp.tile(q_norm, ATT_HEADS).reshape(1, ATT_DIM) * (log2e / math.sqrt(HEAD_DIM))
    return pl.pallas_call(
        _attn_layer_kernel,
        grid=(b, s // ATT_ROWS),
        in_specs=[
            pl.BlockSpec((1, ATT_ROWS, D_MODEL), lambda i, j: (i, j, 0)),
            _resident((1, D_MODEL)),
            _resident((D_MODEL, ATT_DIM)),
            _resident((1, ATT_DIM)),
            _resident(hsum.shape),
            _resident(hspread.shape),
            pl.BlockSpec((1, s + PAD_LEN, ATT_DIM), lambda i, j: (i, 0, 0)),
            pl.BlockSpec((1, ATT_DIM, s + PAD_LEN), lambda i, j: (i, 0, 0)),
            _resident(near.shape),
            _resident(far.shape),
            _resident(farmask.shape),
            _resident((ATT_DIM, D_MODEL)),
        ],
        out_specs=pl.BlockSpec((1, ATT_ROWS, D_MODEL), lambda i, j: (i, j, 0)),
        out_shape=jax.ShapeDtypeStruct((b, s, D_MODEL), F32),
        scratch_shapes=[
            pltpu.VMEM((ATT_ROWS, ATT_DIM), BF16),
            pltpu.VMEM((ATT_DIM, ATT_ROWS), BF16),
        ],
        compiler_params=_params("arbitrary", "arbitrary"),
        name="attn_layer",
    )(x, g.reshape(1, D_MODEL), w_q.astype(BF16), qn, hsum, hspread, k_pad, vt_pad,
      near, far, farmask, w_o.astype(BF16))


def kernel(x, ffn1_norm, ffn1_w_gate, ffn1_w_up, ffn1_w_down, ffn2_norm, ffn2_w_gate, ffn2_w_up,
           ffn2_w_down, ssm_norm, ssm_in_proj, ssm_conv_w, ssm_conv_b, ssm_dt_bias, ssm_A_log, ssm_D,
           ssm_out_norm, ssm_out_proj, kv_norm, w_kv, k_norm, att_norm, att_w_q, att_q_norm,
           att_rel_bias, att_w_o):
    b, s, d = x.shape
    depth = ffn1_norm.shape[0]
    n_ssm = ssm_norm.shape[0]
    t = b * s
    k_pad = vt_pad = None
    ffn1 = [w.astype(BF16) for w in (ffn1_w_gate, ffn1_w_up, ffn1_w_down)]
    ffn2 = [w.astype(BF16) for w in (ffn2_w_gate, ffn2_w_up, ffn2_w_down)]
    ssm_in = ssm_in_proj[:, :, :D_INNER + CONV_DIM].astype(BF16)
    ssm_out = ssm_out_proj.astype(BF16)
    for l in range(depth):
        x = _ffn(x.reshape(t, d), ffn1_norm[l], *ffn1, l).reshape(b, s, d)
        if l < n_ssm:
            x = _ssm(x, ssm_norm[l], ssm_in, ssm_in_proj[l, :, D_INNER + CONV_DIM:], ssm_conv_w[l],
                     ssm_conv_b[l], ssm_dt_bias[l], ssm_A_log[l], ssm_D[l], ssm_out_norm[l], ssm_out, l)
        else:
            if l == n_ssm:
                k_pad, vt_pad = _shared_kv(x, kv_norm, w_kv, k_norm)
            j = l - n_ssm
            x = _attn_layer(x, k_pad, vt_pad, att_norm[j], att_w_q[j], att_q_norm[j],
                            att_rel_bias[j], att_w_o[j])
        x = _ffn(x.reshape(t, d), ffn2_norm[l], *ffn2, l).reshape(b, s, d)
    return x
```

```python
import functools
import math

import jax
import jax.numpy as jnp
from jax import lax
from jax.experimental import pallas as pl
from jax.experimental.pallas import tpu as pltpu

F32 = jnp.float32
BF16 = jnp.bfloat16

D_MODEL = 1024
D_FF = 2816
EPS = 1e-6

D_INNER = 2048
HEAD_DIM = 64
SSM_HEADS = 32
SSM_GROUPS = 8
D_STATE = 128
CONV_W = 4
CONV_DIM = D_INNER + 2 * SSM_GROUPS * D_STATE
GROUP_W = D_INNER // SSM_GROUPS
CHUNK = 64
CONV_COLS = 512

ATT_HEADS = 16
ATT_DIM = 1024
LEFT_CHUNKS = 8
BAND = (LEFT_CHUNKS + 1) * CHUNK
PAD_LEN = LEFT_CHUNKS * CHUNK
MAX_REL = 128

LANES = 128
VMEM_LIMIT = 56 * 1024 * 1024

FFN_ROWS = 512
SSM_ROWS = 256
Q_COLS = 256
SCORE_LOOKAHEAD = 2
ATT_ROWS = 256
ATT_WIN = ATT_ROWS + PAD_LEN
NEAR_ROWS = ATT_WIN - (PAD_LEN - MAX_REL)
FAR_MASK_CHUNKS = ATT_ROWS // CHUNK - 1
TOEP_W = 640


def _resident(shape):
    nd = len(shape)
    return pl.BlockSpec(shape, lambda *_: (0,) * nd, pipeline_mode=pl.Buffered(1))


def _resident_layer(stack_shape, layer):
    rest = tuple(stack_shape[1:])
    return pl.BlockSpec((None,) + rest, lambda *_: (layer,) + (0,) * len(rest),
                        pipeline_mode=pl.Buffered(1))


def _params(*sem):
    return pltpu.CompilerParams(dimension_semantics=sem, vmem_limit_bytes=VMEM_LIMIT)


def _rms(x, g):
    return x * lax.rsqrt(jnp.mean(x * x, axis=-1, keepdims=True) + EPS) * g


def _dot(a, b):
    return jnp.dot(a, b, preferred_element_type=F32)


def _dot_nt(a, b):
    return lax.dot_general(a, b, (((1,), (1,)), ((), ())), preferred_element_type=F32)


def _dot_tn(a, b):
    return lax.dot_general(a, b, (((0,), (0,)), ((), ())), preferred_element_type=F32)


def _split2(v):
    hi = v.astype(BF16)
    lo = (v - hi.astype(F32)).astype(BF16)
    return hi, lo


def _silu(v):
    return v * jax.nn.sigmoid(v)


def _ffn_kernel(x_ref, g_ref, wg_ref, wu_ref, wd_ref, o_ref):
    half = x_ref.shape[0] // 2
    hidden = []
    for r in range(2):
        x = x_ref[r * half:(r + 1) * half, :]
        xn = _rms(x, g_ref[...]).astype(BF16)
        hidden.append((_dot(xn, wg_ref[...]), _dot(xn, wu_ref[...])))
    for r in range(2):
        gate, up = hidden[r]
        h = (_silu(gate) * up).astype(BF16)
        rows = slice(r * half, (r + 1) * half)
        o_ref[rows, :] = x_ref[rows, :] + 0.5 * _dot(h, wd_ref[...])


def _ffn(x2, g, wg, wu, wd, layer):
    t = x2.shape[0]
    tm = min(FFN_ROWS, t)
    return pl.pallas_call(
        _ffn_kernel,
        grid=(t // tm,),
        in_specs=[
            pl.BlockSpec((tm, D_MODEL), lambda i: (i, 0)),
            _resident((1, D_MODEL)),
            _resident_layer(wg.shape, layer),
            _resident_layer(wu.shape, layer),
            _resident_layer(wd.shape, layer),
        ],
        out_specs=pl.BlockSpec((tm, D_MODEL), lambda i: (i, 0)),
        out_shape=jax.ShapeDtypeStruct((t, D_MODEL), F32),
        compiler_params=_params("arbitrary"),
        name="ffn",
    )(x2, g.reshape(1, D_MODEL), wg, wu, wd)


def _ssm_kernel(x_ref, g_ref, win_ref, wdt_ref, cw_ref, cb_ref, dtb_ref,
                alog_ref, alogf_ref, dskip_ref, onorm_ref, wout_ref,
                expand_ref, triu_ref, negmask_ref, tril_ref, o_ref,
                h_scr, tail_scr, xpad_scr, xs_scr, bc_scr, z_scr, dtf_scr, acsf_scr, y_scr):
    ts = x_ref.shape[1]

    @pl.when(pl.program_id(1) == 0)
    def _():
        h_scr[...] = jnp.zeros_like(h_scr)
        tail_scr[...] = jnp.zeros_like(tail_scr)

    x = x_ref[0]
    xn = _rms(x, g_ref[...]).astype(BF16)

    n_grp = CONV_DIM // CONV_COLS
    n_xs = D_INNER // CONV_COLS

    def project(g):
        return _dot(xn, win_ref[:, D_INNER + g * CONV_COLS:D_INNER + (g + 1) * CONV_COLS])

    def conv_group(g, pre):
        cols = slice(g * CONV_COLS, (g + 1) * CONV_COLS)
        slot = g % 2
        xpad_scr[slot, 0:8, :] = tail_scr[:, cols]
        xpad_scr[slot, 8:, :] = pre
        tail_scr[:, cols] = pre[ts - 8:, :]
        conv = cb_ref[:, cols] + cw_ref[CONV_W - 1:CONV_W, cols] * pre
        for k in range(CONV_W - 1):
            off = 8 - (CONV_W - 1) + k
            conv = conv + cw_ref[k:k + 1, cols] * xpad_scr[slot, off:off + ts, :]
        act = _silu(conv)
        if g < n_xs:
            xs_scr[:, cols] = act
        else:
            bc_scr[:, (g - n_xs) * CONV_COLS:(g - n_xs + 1) * CONV_COLS] = act.astype(BF16)

    a_head = -jnp.exp(alog_ref[...])
    a_full = -jnp.exp(alogf_ref[...])

    def expand(v):
        hi, lo = _split2(v)
        return _dot(jnp.concatenate([hi, lo], axis=1), expand_ref[...])

    pre_next = project(0)
    for g in range(n_grp):
        pre = pre_next
        if g + 1 < n_grp:
            pre_next = project(g + 1)
        if g % 2 == 1:
            zc = slice((g // 2) * CONV_COLS, (g // 2 + 1) * CONV_COLS)
            z_scr[:, zc] = _dot(xn, win_ref[:, zc])
        if g == 0:
            dt = jax.nn.softplus(_dot(xn, wdt_ref[...]) + dtb_ref[...])
        elif g == n_grp // 4:
            dtf_scr[...] = expand(dt)
        elif g == n_grp // 2:
            a = dt * a_head
            hi = a.astype(BF16)
            r1 = a - hi.astype(F32)
            mid = r1.astype(BF16)
            lo = (r1 - mid.astype(F32)).astype(BF16)
            acs = _dot(tril_ref[...], jnp.concatenate([hi, mid, lo], axis=0))
        elif g == 3 * n_grp // 4:
            acsf_scr[...] = expand(acs)
        conv_group(g, pre)

    lane_head = lax.broadcasted_iota(jnp.int32, (1, GROUP_W), 1) // HEAD_DIM
    head_lanes = [(lane_head == r).astype(BF16) for r in range(GROUP_W // HEAD_DIM)]
    groups = range(SSM_GROUPS)
    gls = [slice(g * GROUP_W, (g + 1) * GROUP_W) for g in groups]

    def chunk_body(c, carry):
        r0 = pl.multiple_of(c * CHUNK, CHUNK)
        rows = pl.ds(r0, CHUNK)
        b_bf = [bc_scr[rows, g * D_STATE:(g + 1) * D_STATE] for g in groups]
        c_bf = [bc_scr[rows, (SSM_GROUPS + g) * D_STATE:(SSM_GROUPS + g + 1) * D_STATE] for g in groups]
        cb4 = [_dot_nt(c_bf[g], jnp.concatenate([b_bf[g]] * 4, axis=0)) for g in groups]

        y_off = [_dot(c_bf[g], h_scr[:, gls[g]].astype(BF16)) for g in groups]
        acs_last = acsf_scr[pl.ds(r0 + CHUNK - 1, 1), :]
        dec_chunk = jnp.exp(acs_last)
        for g in groups:
            gl = gls[g]
            acs_g = acsf_scr[rows, gl]
            xdec = (xs_scr[rows, gl] * dtf_scr[rows, gl] * jnp.exp(acs_last[:, gl] - acs_g)).astype(BF16)
            s_g = _dot_tn(b_bf[g], xdec)
            h_scr[:, gl] = h_scr[:, gl] * dec_chunk[:, gl] + s_g

        for g in groups:
            gl = gls[g]
            acs_g = acsf_scr[rows, gl]
            dt_g = dtf_scr[rows, gl]
            xs_g = xs_scr[rows, gl]
            acs_t = jnp.sum(dt_g * a_full[:, gl] * triu_ref[:, gl], axis=0, keepdims=True)
            lmat = jnp.exp(acs_g - acs_t + negmask_ref[:, gl])
            w = (cb4[g] * lmat).astype(BF16)
            x_g = (xs_g * dt_g).astype(BF16)
            bd = jnp.concatenate([x_g * m for m in head_lanes], axis=0)
            y_scr[rows, gl] = (_dot(w, bd) + y_off[g] * jnp.exp(acs_g) + dskip_ref[:, gl] * xs_g)
        return carry

    lax.fori_loop(0, ts // CHUNK, chunk_body, 0, unroll=True)

    out = x
    half = D_INNER // 2
    for hf in range(2):
        parts = []
        for g in range(hf * SSM_GROUPS // 2, (hf + 1) * SSM_GROUPS // 2):
            gg = y_scr[:, gls[g]] * _silu(z_scr[:, gls[g]])
            parts.append(gg * lax.rsqrt(jnp.mean(gg * gg, axis=-1, keepdims=True) + EPS))
        cols = slice(hf * half, (hf + 1) * half)
        yn = (jnp.concatenate(parts, axis=1) * onorm_ref[:, cols]).astype(BF16)
        out = out + _dot(yn, wout_ref[cols, :])
    o_ref[0] = out


def _ssm_constants(ts):
    h_of_lane = jnp.arange(D_INNER) // HEAD_DIM
    e0 = (jnp.arange(LANES)[:, None] == h_of_lane[None, :]).astype(BF16)
    expand = jnp.concatenate([e0, e0], axis=0)
    t_lane = jnp.arange(D_INNER) % HEAD_DIM
    t_row = jnp.arange(CHUNK)
    triu = (t_row[:, None] <= t_lane[None, :]).astype(F32)
    negmask = jnp.where(t_lane[None, :] <= t_row[:, None], 0.0, -jnp.inf).astype(F32)
    r = jnp.arange(ts)
    tril = ((r[None, :] <= r[:, None]) & (r[None, :] // CHUNK == r[:, None] // CHUNK)).astype(BF16)
    return expand, triu, negmask, jnp.concatenate([tril] * 3, axis=1)


def _ssm(x, g, w_in_stack, w_dt, conv_w, conv_b, dt_bias, a_log, d_skip, out_norm, w_out_stack, layer):
    b, s, _ = x.shape
    ts = min(SSM_ROWS, s)
    wdt = jnp.pad(w_dt, ((0, 0), (0, LANES - SSM_HEADS))).astype(BF16)
    pad_h = lambda v: jnp.pad(v.reshape(1, SSM_HEADS), ((0, 0), (0, LANES - SSM_HEADS)))
    rep = lambda v: jnp.repeat(v, HEAD_DIM).reshape(1, D_INNER)
    expand, triu, negmask, tril = _ssm_constants(ts)
    operands = (
        x, g.reshape(1, D_MODEL), w_in_stack, wdt, conv_w, conv_b.reshape(1, CONV_DIM),
        pad_h(dt_bias), pad_h(a_log), rep(a_log), rep(d_skip), out_norm.reshape(1, D_INNER),
        w_out_stack, expand, triu, negmask, tril)
    in_specs = [pl.BlockSpec((1, ts, D_MODEL), lambda i, j: (i, j, 0))]
    in_specs += [_resident_layer(op.shape, layer) if op is w_in_stack or op is w_out_stack
                 else _resident(op.shape) for op in operands[1:]]
    return pl.pallas_call(
        _ssm_kernel,
        grid=(b, s // ts),
        in_specs=in_specs,
        out_specs=pl.BlockSpec((1, ts, D_MODEL), lambda i, j: (i, j, 0)),
        out_shape=jax.ShapeDtypeStruct((b, s, D_MODEL), F32),
        scratch_shapes=[
            pltpu.VMEM((D_STATE, D_INNER), F32),
            pltpu.VMEM((8, CONV_DIM), F32),
            pltpu.VMEM((2, ts + 8, CONV_COLS), F32),
            pltpu.VMEM((ts, D_INNER), F32),
            pltpu.VMEM((ts, CONV_DIM - D_INNER), BF16),
            pltpu.VMEM((ts, D_INNER), F32),
            pltpu.VMEM((ts, D_INNER), F32),
            pltpu.VMEM((ts, D_INNER), F32),
            pltpu.VMEM((ts, D_INNER), F32),
        ],
        compiler_params=_params("arbitrary", "arbitrary"),
        name="ssm",
    )(*operands)


def _head_norm_constants(width=ATT_DIM):
    c = jnp.arange(width) // HEAD_DIM
    s0 = (c[:, None] == jnp.arange(LANES)[None, :]).astype(BF16)
    return jnp.concatenate([s0, s0], axis=0), jnp.concatenate([s0.T, s0.T], axis=0)


def _kv_kernel(x_ref, g_ref, wk_ref, wvt_ref, kn_ref, sum_ref, spread_ref, k_ref, vt_ref):
    i = pl.program_id(1)

    @pl.when(i == 0)
    def _():
        k_ref[...] = jnp.zeros_like(k_ref)
        vt_ref[...] = jnp.zeros_like(vt_ref)

    @pl.when(i > 0)
    def _():
        half = x_ref.shape[1] // 2
        k_raw, mean_sq = [], []
        for r in range(2):
            rows = slice(r * half, (r + 1) * half)
            xn = _rms(x_ref[0, rows, :], g_ref[...]).astype(BF16)
            k_raw.append(_dot(xn, wk_ref[...]))
            vt_ref[0, :, rows] = _dot_nt(wvt_ref[...], xn).astype(BF16)
        for r in range(2):
            hi, lo = _split2(k_raw[r] * k_raw[r])
            mean_sq.append(_dot(jnp.concatenate([hi, lo], axis=1), sum_ref[...]) * (1.0 / HEAD_DIM))
        for r in range(2):
            rh, rl = _split2(lax.rsqrt(mean_sq[r] + EPS))
            inv = _dot(jnp.concatenate([rh, rl], axis=1), spread_ref[...])
            k_ref[0, r * half:(r + 1) * half, :] = (k_raw[r] * inv * kn_ref[...]).astype(BF16)


def _shared_kv(x, g, w_kv, k_norm):
    b, s, _ = x.shape
    tk = PAD_LEN
    hsum, hspread = _head_norm_constants()
    return pl.pallas_call(
        _kv_kernel,
        grid=(b, s // tk + 1),
        in_specs=[
            pl.BlockSpec((1, tk, D_MODEL), lambda i, j: (i, jnp.maximum(j - 1, 0), 0)),
            _resident((1, D_MODEL)),
            _resident((D_MODEL, ATT_DIM)),
            _resident((ATT_DIM, D_MODEL)),
            _resident((1, ATT_DIM)),
            _resident(hsum.shape),
            _resident(hspread.shape),
        ],
        out_specs=[pl.BlockSpec((1, tk, ATT_DIM), lambda i, j: (i, j, 0)),
                   pl.BlockSpec((1, ATT_DIM, tk), lambda i, j: (i, 0, j))],
        out_shape=[jax.ShapeDtypeStruct((b, s + PAD_LEN, ATT_DIM), BF16),
                   jax.ShapeDtypeStruct((b, ATT_DIM, s + PAD_LEN), BF16)],
        compiler_params=_params("arbitrary", "arbitrary"),
        name="shared_kv",
    )(x, g.reshape(1, D_MODEL), w_kv[:, :ATT_DIM].astype(BF16), w_kv[:, ATT_DIM:].T.astype(BF16),
      jnp.tile(k_norm, ATT_HEADS).reshape(1, ATT_DIM), hsum, hspread)


def _bias_kernel(w_ref, o_ref):
    rows = jnp.broadcast_to(w_ref[0], (NEAR_ROWS, TOEP_W))
    toep = pltpu.roll(rows, 0, 1, stride=1, stride_axis=0)[:, 0:ATT_ROWS]
    key_chunk = lax.broadcasted_iota(jnp.int32, (NEAR_ROWS, ATT_ROWS), 0) // CHUNK + (ATT_WIN - NEAR_ROWS) // CHUNK
    query_chunk = lax.broadcasted_iota(jnp.int32, (NEAR_ROWS, ATT_ROWS), 1) // CHUNK
    o_ref[0] = jnp.where(key_chunk - query_chunk <= LEFT_CHUNKS, toep, -jnp.inf)


def _rel_bias_tables(rel_bias):
    far = rel_bias[:, :1]
    w = jnp.concatenate([
        jnp.broadcast_to(far, (ATT_HEADS, ATT_ROWS + 1)),
        jnp.broadcast_to(rel_bias[:, -1:], (ATT_HEADS, TOEP_W - ATT_ROWS - 1 - 2 * MAX_REL)),
        rel_bias[:, :0:-1],
    ], axis=1).reshape(ATT_HEADS, 1, TOEP_W)
    near = pl.pallas_call(
        _bias_kernel,
        grid=(ATT_HEADS,),
        in_specs=[pl.BlockSpec((1, 1, TOEP_W), lambda h: (h, 0, 0))],
        out_specs=pl.BlockSpec((1, NEAR_ROWS, ATT_ROWS), lambda h: (h, 0, 0)),
        out_shape=jax.ShapeDtypeStruct((ATT_HEADS, NEAR_ROWS, ATT_ROWS), F32),
        compiler_params=_params("arbitrary"),
        name="rel_bias",
    )(w)
    return near, jnp.broadcast_to(far[:, :, None], (ATT_HEADS, 1, ATT_ROWS))


def _attn_layer_kernel(x_ref, g_ref, wq_ref, qn_ref, sum_ref, spread_ref, k_ref, vt_ref,
                       near_ref, far_ref, farmask_ref, wo_ref, o_ref, q_scr, ot_scr):
    base = pl.multiple_of(pl.program_id(1) * ATT_ROWS, ATT_ROWS)
    x = x_ref[0]
    xn = _rms(x, g_ref[...]).astype(BF16)

    def q_project(g):
        return _dot(xn, wq_ref[:, g * Q_COLS:(g + 1) * Q_COLS])

    def q_mean_sq(qraw):
        hi, lo = _split2(qraw * qraw)
        return _dot(jnp.concatenate([hi, lo], axis=1), sum_ref[...]) * (1.0 / HEAD_DIM)

    def q_inv_rms(ms):
        rh, rl = _split2(lax.rsqrt(ms + EPS))
        return _dot(jnp.concatenate([rh, rl], axis=1), spread_ref[...])

    def q_store(g, qraw, inv):
        cols = slice(g * Q_COLS, (g + 1) * Q_COLS)
        q_scr[:, cols] = (qraw * inv * qn_ref[:, cols]).astype(BF16)

    qraw = q_project(0)
    q_store(0, qraw, q_inv_rms(q_mean_sq(qraw)))

    lane = lax.broadcasted_iota(jnp.int32, (1, LANES), 1)
    head_mask = [(lane < HEAD_DIM).astype(BF16), (lane >= HEAD_DIM).astype(BF16)]
    n_kc = ATT_WIN // CHUNK
    far_kc = (ATT_WIN - NEAR_ROWS) // CHUNK
    pad_neg = [jnp.where(base + kc * CHUNK < PAD_LEN, -jnp.inf, 0.0).astype(F32)
               for kc in range(PAD_LEN // CHUNK)]

    chunks_per_half = LANES // CHUNK
    seen_kc = LEFT_CHUNKS + chunks_per_half

    def scores_t(h):
        l0 = (h // 2) * LANES
        q_pair = q_scr[:, pl.ds(l0, LANES)]
        k_win = k_ref[0, pl.ds(base, ATT_WIN), pl.ds(l0, LANES)]
        st = _dot_nt(k_win, q_pair * head_mask[h % 2])
        far_row = far_ref[h]
        halves = []
        for hf in range(ATT_ROWS // LANES):
            lanes = slice(hf * LANES, (hf + 1) * LANES)
            first_kc = hf * chunks_per_half
            blocks = []
            for kc in range(first_kc, first_kc + seen_kc):
                rows = slice(kc * CHUNK, (kc + 1) * CHUNK)
                blk = st[rows, lanes]
                if kc < far_kc:
                    blk = blk + (far_row[:, lanes] + pad_neg[kc])
                    if kc == first_kc:
                        blk = blk + farmask_ref[rows, lanes]
                else:
                    blk = blk + near_ref[h, (kc - far_kc) * CHUNK:(kc - far_kc + 1) * CHUNK, lanes]
                    if kc < len(pad_neg):
                        blk = blk + pad_neg[kc]
                blocks.append(blk)
            s = jnp.concatenate(blocks, axis=0)
            halves.append((s, jnp.max(s, axis=0, keepdims=True)))
        return halves

    heads_per_group = Q_COLS // HEAD_DIM
    half_dim = ATT_DIM // 2
    out = x
    qraw = q_project(1)
    pending = [scores_t(h) for h in range(SCORE_LOOKAHEAD)]
    for h in range(ATT_HEADS):
        halves = pending.pop(0)
        g_next, stage = divmod(h + SCORE_LOOKAHEAD + heads_per_group - 1, heads_per_group)
        if 1 <= g_next < ATT_DIM // Q_COLS:
            if stage == 0:
                qraw = q_project(g_next)
            elif stage == 1:
                ms = q_mean_sq(qraw)
            elif stage == 2:
                inv = q_inv_rms(ms)
            else:
                q_store(g_next, qraw, inv)
        if h + SCORE_LOOKAHEAD < ATT_HEADS:
            pending.append(scores_t(h + SCORE_LOOKAHEAD))
        p_cols, denoms = [], []
        for hf, (s, s_max) in enumerate(halves):
            p_half = jnp.exp2(s - s_max)
            denoms.append(jnp.sum(p_half, axis=0, keepdims=True))
            above = jnp.zeros((hf * chunks_per_half * CHUNK, LANES), BF16)
            below = jnp.zeros((ATT_WIN - (hf * chunks_per_half + seen_kc) * CHUNK, LANES), BF16)
            p_cols.append(jnp.concatenate(
                [blk for blk in (above, p_half.astype(BF16), below) if blk.shape[0]], axis=0))
        p = jnp.concatenate(p_cols, axis=1)
        denom = jnp.concatenate(denoms, axis=1)
        r0 = h * HEAD_DIM
        vt = vt_ref[0, pl.ds(r0, HEAD_DIM), pl.ds(base, ATT_WIN)]
        ot = _dot(vt, p) / denom
        ot_scr[pl.ds(r0, HEAD_DIM), :] = ot.astype(BF16)
        if h == ATT_HEADS // 2:
            out = out + _dot_tn(ot_scr[0:half_dim, :], wo_ref[0:half_dim, :])
    o_ref[0] = out + _dot_tn(ot_scr[half_dim:, :], wo_ref[half_dim:, :])


def _attn_layer(x, k_pad, vt_pad, g, w_q, q_norm, rel_bias, w_o):
    b, s, _ = x.shape
    hsum, hspread = _head_norm_constants(Q_COLS)
    log2e = 1.0 / math.log(2.0)
    near, far = _rel_bias_tables(rel_bias * log2e)
    key_chunk = jnp.arange(FAR_MASK_CHUNKS * CHUNK) // CHUNK
    query_chunk = jnp.arange(ATT_ROWS) // CHUNK
    farmask = jnp.where(key_chunk[:, None] >= query_chunk[None, :], 0.0, -jnp.inf).astype(F32)
    qn = jnp.tile(q_norm, ATT_HEADS).reshape(1, ATT_DIM) * (log2e / math.sqrt(HEAD_DIM))
    return pl.pallas_call(
        _attn_layer_kernel,
        grid=(b, s // ATT_ROWS),
        in_specs=[
            pl.BlockSpec((1, ATT_ROWS, D_MODEL), lambda i, j: (i, j, 0)),
            _resident((1, D_MODEL)),
            _resident((D_MODEL, ATT_DIM)),
            _resident((1, ATT_DIM)),
            _resident(hsum.shape),
            _resident(hspread.shape),
            pl.BlockSpec((1, s + PAD_LEN, ATT_DIM), lambda i, j: (i, 0, 0)),
            pl.BlockSpec((1, ATT_DIM, s + PAD_LEN), lambda i, j: (i, 0, 0)),
            _resident(near.shape),
            _resident(far.shape),
            _resident(farmask.shape),
            _resident((ATT_DIM, D_MODEL)),
        ],
        out_specs=pl.BlockSpec((1, ATT_ROWS, D_MODEL), lambda i, j: (i, j, 0)),
        out_shape=jax.ShapeDtypeStruct((b, s, D_MODEL), F32),
        scratch_shapes=[
            pltpu.VMEM((ATT_ROWS, ATT_DIM), BF16),
            pltpu.VMEM((ATT_DIM, ATT_ROWS), BF16),
        ],
        compiler_params=_params("arbitrary", "arbitrary"),
        name="attn_layer",
    )(x, g.reshape(1, D_MODEL), w_q.astype(BF16), qn, hsum, hspread, k_pad, vt_pad,
      near, far, farmask, w_o.astype(BF16))


def kernel(x, ffn1_norm, ffn1_w_gate, ffn1_w_up, ffn1_w_down, ffn2_norm, ffn2_w_gate, ffn2_w_up,
           ffn2_w_down, ssm_norm, ssm_in_proj, ssm_conv_w, ssm_conv_b, ssm_dt_bias, ssm_A_log, ssm_D,
           ssm_out_norm, ssm_out_proj, kv_norm, w_kv, k_norm, att_norm, att_w_q, att_q_norm,
           att_rel_bias, att_w_o):
    b, s, d = x.shape
    depth = ffn1_norm.shape[0]
    n_ssm = ssm_norm.shape[0]
    t = b * s
    k_pad = vt_pad = None
    ffn1 = [w.astype(BF16) for w in (ffn1_w_gate, ffn1_w_up, ffn1_w_down)]
    ffn2 = [w.astype(BF16) for w in (ffn2_w_gate, ffn2_w_up, ffn2_w_down)]
    ssm_in = ssm_in_proj[:, :, :D_INNER + CONV_DIM].astype(BF16)
    ssm_out = ssm_out_proj.astype(BF16)
    for l in range(depth):
        x = _ffn(x.reshape(t, d), ffn1_norm[l], *ffn1, l).reshape(b, s, d)
        if l < n_ssm:
            x = _ssm(x, ssm_norm[l], ssm_in, ssm_in_proj[l, :, D_INNER + CONV_DIM:], ssm_conv_w[l],
                     ssm_conv_b[l], ssm_dt_bias[l], ssm_A_log[l], ssm_D[l], ssm_out_norm[l], ssm_out, l)
        else:
            if l == n_ssm:
                k_pad, vt_pad = _shared_kv(x, kv_norm, w_kv, k_norm)
            j = l - n_ssm
            x = _attn_layer(x, k_pad, vt_pad, att_norm[j], att_w_q[j], att_q_norm[j],
                            att_rel_bias[j], att_w_o[j])
        x = _ffn(x.reshape(t, d), ffn2_norm[l], *ffn2, l).reshape(b, s, d)
    return x
```

```python
import functools
import math

import jax
import jax.numpy as jnp
from jax import lax
from jax.experimental import pallas as pl
from jax.experimental.pallas import tpu as pltpu

F32 = jnp.float32
BF16 = jnp.bfloat16

D_MODEL = 1024
D_FF = 2816
EPS = 1e-6

D_INNER = 2048
HEAD_DIM = 64
SSM_HEADS = 32
SSM_GROUPS = 8
D_STATE = 128
CONV_W = 4
CONV_DIM = D_INNER + 2 * SSM_GROUPS * D_STATE
GROUP_W = D_INNER // SSM_GROUPS
CHUNK = 64
CONV_COLS = 512

ATT_HEADS = 16
ATT_DIM = 1024
LEFT_CHUNKS = 8
BAND = (LEFT_CHUNKS + 1) * CHUNK
PAD_LEN = LEFT_CHUNKS * CHUNK
MAX_REL = 128

LANES = 128
VMEM_LIMIT = 56 * 1024 * 1024

FFN_ROWS = 512
SSM_ROWS = 256
Q_COLS = 256
SCORE_LOOKAHEAD = 2
ATT_ROWS = 256
ATT_WIN = ATT_ROWS + PAD_LEN
NEAR_ROWS = ATT_WIN - (PAD_LEN - MAX_REL)
FAR_MASK_CHUNKS = ATT_ROWS // CHUNK - 1
TOEP_W = 640


def _resident(shape):
    nd = len(shape)
    return pl.BlockSpec(shape, lambda *_: (0,) * nd, pipeline_mode=pl.Buffered(1))


def _resident_layer(stack_shape, layer):
    rest = tuple(stack_shape[1:])
    return pl.BlockSpec((None,) + rest, lambda *_: (layer,) + (0,) * len(rest),
                        pipeline_mode=pl.Buffered(1))


def _params(*sem):
    return pltpu.CompilerParams(dimension_semantics=sem, vmem_limit_bytes=VMEM_LIMIT)


def _rms(x, g):
    return x * lax.rsqrt(jnp.mean(x * x, axis=-1, keepdims=True) + EPS) * g


def _dot(a, b):
    return jnp.dot(a, b, preferred_element_type=F32)


def _dot_nt(a, b):
    return lax.dot_general(a, b, (((1,), (1,)), ((), ())), preferred_element_type=F32)


def _dot_tn(a, b):
    return lax.dot_general(a, b, (((0,), (0,)), ((), ())), preferred_element_type=F32)


def _split2(v):
    hi = v.astype(BF16)
    lo = (v - hi.astype(F32)).astype(BF16)
    return hi, lo


def _silu(v):
    return v * jax.nn.sigmoid(v)


def _ffn_kernel(x_ref, g_ref, wg_ref, wu_ref, wd_ref, o_ref):
    half = x_ref.shape[0] // 2
    hidden = []
    for r in range(2):
        x = x_ref[r * half:(r + 1) * half, :]
        xn = _rms(x, g_ref[...]).astype(BF16)
        hidden.append((_dot(xn, wg_ref[...]), _dot(xn, wu_ref[...])))
    for r in range(2):
        gate, up = hidden[r]
        h = (_silu(gate) * up).astype(BF16)
        rows = slice(r * half, (r + 1) * half)
        o_ref[rows, :] = x_ref[rows, :] + 0.5 * _dot(h, wd_ref[...])


def _ffn(x2, g, wg, wu, wd, layer):
    t = x2.shape[0]
    tm = min(FFN_ROWS, t)
    return pl.pallas_call(
        _ffn_kernel,
        grid=(t // tm,),
        in_specs=[
            pl.BlockSpec((tm, D_MODEL), lambda i: (i, 0)),
            _resident((1, D_MODEL)),
            _resident_layer(wg.shape, layer),
            _resident_layer(wu.shape, layer),
            _resident_layer(wd.shape, layer),
        ],
        out_specs=pl.BlockSpec((tm, D_MODEL), lambda i: (i, 0)),
        out_shape=jax.ShapeDtypeStruct((t, D_MODEL), F32),
        compiler_params=_params("arbitrary"),
        name="ffn",
    )(x2, g.reshape(1, D_MODEL), wg, wu, wd)


def _ssm_kernel(x_ref, g_ref, win_ref, wdt_ref, cw_ref, cb_ref, dtb_ref,
                alog_ref, alogf_ref, dskip_ref, onorm_ref, wout_ref,
                expand_ref, triu_ref, negmask_ref, tril_ref, o_ref,
                h_scr, tail_scr, xpad_scr, xs_scr, bc_scr, z_scr, dtf_scr, acsf_scr, y_scr):
    ts = x_ref.shape[1]

    @pl.when(pl.program_id(1) == 0)
    def _():
        h_scr[...] = jnp.zeros_like(h_scr)
        tail_scr[...] = jnp.zeros_like(tail_scr)

    x = x_ref[0]
    xn = _rms(x, g_ref[...]).astype(BF16)

    n_grp = CONV_DIM // CONV_COLS
    n_xs = D_INNER // CONV_COLS

    def project(g):
        return _dot(xn, win_ref[:, D_INNER + g * CONV_COLS:D_INNER + (g + 1) * CONV_COLS])

    def conv_group(g, pre):
        cols = slice(g * CONV_COLS, (g + 1) * CONV_COLS)
        slot = g % 2
        xpad_scr[slot, 0:8, :] = tail_scr[:, cols]
        xpad_scr[slot, 8:, :] = pre
        tail_scr[:, cols] = pre[ts - 8:, :]
        conv = cb_ref[:, cols] + cw_ref[CONV_W - 1:CONV_W, cols] * pre
        for k in range(CONV_W - 1):
            off = 8 - (CONV_W - 1) + k
            conv = conv + cw_ref[k:k + 1, cols] * xpad_scr[slot, off:off + ts, :]
        act = _silu(conv)
        if g < n_xs:
            xs_scr[:, cols] = act
        else:
            bc_scr[:, (g - n_xs) * CONV_COLS:(g - n_xs + 1) * CONV_COLS] = act.astype(BF16)

    a_head = -jnp.exp(alog_ref[...])
    a_full = -jnp.exp(alogf_ref[...])

    def expand(v):
        hi, lo = _split2(v)
        return _dot(jnp.concatenate([hi, lo], axis=1), expand_ref[...])

    pre_next = project(0)
    for g in range(n_grp):
        pre = pre_next
        if g + 1 < n_grp:
            pre_next = project(g + 1)
        if g % 2 == 1:
            zc = slice((g // 2) * CONV_COLS, (g // 2 + 1) * CONV_COLS)
            z_scr[:, zc] = _dot(xn, win_ref[:, zc])
        if g == 0:
            dt = jax.nn.softplus(_dot(xn, wdt_ref[...]) + dtb_ref[...])
        elif g == n_grp // 4:
            dtf_scr[...] = expand(dt)
        elif g == n_grp // 2:
            a = dt * a_head
            hi = a.astype(BF16)
            r1 = a - hi.astype(F32)
            mid = r1.astype(BF16)
            lo = (r1 - mid.astype(F32)).astype(BF16)
            acs = _dot(tril_ref[...], jnp.concatenate([hi, mid, lo], axis=0))
        elif g == 3 * n_grp // 4:
            acsf_scr[...] = expand(acs)
        conv_group(g, pre)

    lane_head = lax.broadcasted_iota(jnp.int32, (1, GROUP_W), 1) // HEAD_DIM
    head_lanes = [(lane_head == r).astype(BF16) for r in range(GROUP_W // HEAD_DIM)]
    groups = range(SSM_GROUPS)
    gls = [slice(g * GROUP_W, (g + 1) * GROUP_W) for g in groups]

    def chunk_body(c, carry):
        r0 = pl.multiple_of(c * CHUNK, CHUNK)
        rows = pl.ds(r0, CHUNK)
        b_bf = [bc_scr[rows, g * D_STATE:(g + 1) * D_STATE] for g in groups]
        c_bf = [bc_scr[rows, (SSM_GROUPS + g) * D_STATE:(SSM_GROUPS + g + 1) * D_STATE] for g in groups]
        cb4 = [_dot_nt(c_bf[g], jnp.concatenate([b_bf[g]] * 4, axis=0)) for g in groups]

        y_off = [_dot(c_bf[g], h_scr[:, gls[g]].astype(BF16)) for g in groups]
        acs_last = acsf_scr[pl.ds(r0 + CHUNK - 1, 1), :]
        dec_chunk = jnp.exp(acs_last)
        for g in groups:
            gl = gls[g]
            acs_g = acsf_scr[rows, gl]
            xdec = (xs_scr[rows, gl] * dtf_scr[rows, gl] * jnp.exp(acs_last[:, gl] - acs_g)).astype(BF16)
            s_g = _dot_tn(b_bf[g], xdec)
            h_scr[:, gl] = h_scr[:, gl] * dec_chunk[:, gl] + s_g

        for g in groups:
            gl = gls[g]
            acs_g = acsf_scr[rows, gl]
            dt_g = dtf_scr[rows, gl]
            xs_g = xs_scr[rows, gl]
            acs_t = jnp.sum(dt_g * a_full[:, gl] * triu_ref[:, gl], axis=0, keepdims=True)
            lmat = jnp.exp(acs_g - acs_t + negmask_ref[:, gl])
            w = (cb4[g] * lmat).astype(BF16)
            x_g = (xs_g * dt_g).astype(BF16)
            bd = jnp.concatenate([x_g * m for m in head_lanes], axis=0)
            y_scr[rows, gl] = (_dot(w, bd) + y_off[g] * jnp.exp(acs_g) + dskip_ref[:, gl] * xs_g)
        return carry

    lax.fori_loop(0, ts // CHUNK, chunk_body, 0, unroll=True)

    out = x
    half = D_INNER // 2
    for hf in range(2):
        parts = []
        for g in range(hf * SSM_GROUPS // 2, (hf + 1) * SSM_GROUPS // 2):
            gg = y_scr[:, gls[g]] * _silu(z_scr[:, gls[g]])
            parts.append(gg * lax.rsqrt(jnp.mean(gg * gg, axis=-1, keepdims=True) + EPS))
        cols = slice(hf * half, (hf + 1) * half)
        yn = (jnp.concatenate(parts, axis=1) * onorm_ref[:, cols]).astype(BF16)
        out = out + _dot(yn, wout_ref[cols, :])
    o_ref[0] = out


def _ssm_constants(ts):
    h_of_lane = jnp.arange(D_INNER) // HEAD_DIM
    e0 = (jnp.arange(LANES)[:, None] == h_of_lane[None, :]).astype(BF16)
    expand = jnp.concatenate([e0, e0], axis=0)
    t_lane = jnp.arange(D_INNER) % HEAD_DIM
    t_row = jnp.arange(CHUNK)
    triu = (t_row[:, None] <= t_lane[None, :]).astype(F32)
    negmask = jnp.where(t_lane[None, :] <= t_row[:, None], 0.0, -jnp.inf).astype(F32)
    r = jnp.arange(ts)
    tril = ((r[None, :] <= r[:, None]) & (r[None, :] // CHUNK == r[:, None] // CHUNK)).astype(BF16)
    return expand, triu, negmask, jnp.concatenate([tril] * 3, axis=1)


def _ssm(x, g, w_in_stack, w_dt, conv_w, conv_b, dt_bias, a_log, d_skip, out_norm, w_out_stack, layer):
    b, s, _ = x.shape
    ts = min(SSM_ROWS, s)
    wdt = jnp.pad(w_dt, ((0, 0), (0, LANES - SSM_HEADS))).astype(BF16)
    pad_h = lambda v: jnp.pad(v.reshape(1, SSM_HEADS), ((0, 0), (0, LANES - SSM_HEADS)))
    rep = lambda v: jnp.repeat(v, HEAD_DIM).reshape(1, D_INNER)
    expand, triu, negmask, tril = _ssm_constants(ts)
    operands = (
        x, g.reshape(1, D_MODEL), w_in_stack, wdt, conv_w, conv_b.reshape(1, CONV_DIM),
        pad_h(dt_bias), pad_h(a_log), rep(a_log), rep(d_skip), out_norm.reshape(1, D_INNER),
        w_out_stack, expand, triu, negmask, tril)
    in_specs = [pl.BlockSpec((1, ts, D_MODEL), lambda i, j: (i, j, 0))]
    in_specs += [_resident_layer(op.shape, layer) if op is w_in_stack or op is w_out_stack
                 else _resident(op.shape) for op in operands[1:]]
    return pl.pallas_call(
        _ssm_kernel,
        grid=(b, s // ts),
        in_specs=in_specs,
        out_specs=pl.BlockSpec((1, ts, D_MODEL), lambda i, j: (i, j, 0)),
        out_shape=jax.ShapeDtypeStruct((b, s, D_MODEL), F32),
        scratch_shapes=[
            pltpu.VMEM((D_STATE, D_INNER), F32),
            pltpu.VMEM((8, CONV_DIM), F32),
            pltpu.VMEM((2, ts + 8, CONV_COLS), F32),
            pltpu.VMEM((ts, D_INNER), F32),
            pltpu.VMEM((ts, CONV_DIM - D_INNER), BF16),
            pltpu.VMEM((ts, D_INNER), F32),
            pltpu.VMEM((ts, D_INNER), F32),
            pltpu.VMEM((ts, D_INNER), F32),
            pltpu.VMEM((ts, D_INNER), F32),
        ],
        compiler_params=_params("arbitrary", "arbitrary"),
        name="ssm",
    )(*operands)


def _head_norm_constants(width=ATT_DIM):
    c = jnp.arange(width) // HEAD_DIM
    s0 = (c[:, None] == jnp.arange(LANES)[None, :]).astype(BF16)
    return jnp.concatenate([s0, s0], axis=0), jnp.concatenate([s0.T, s0.T], axis=0)


def _kv_kernel(x_ref, g_ref, wk_ref, wvt_ref, kn_ref, sum_ref, spread_ref, k_ref, vt_ref):
    i = pl.program_id(1)

    @pl.when(i == 0)
    def _():
        k_ref[...] = jnp.zeros_like(k_ref)
        vt_ref[...] = jnp.zeros_like(vt_ref)

    @pl.when(i > 0)
    def _():
        half = x_ref.shape[1] // 2
        k_raw, mean_sq = [], []
        for r in range(2):
            rows = slice(r * half, (r + 1) * half)
            xn = _rms(x_ref[0, rows, :], g_ref[...]).astype(BF16)
            k_raw.append(_dot(xn, wk_ref[...]))
            vt_ref[0, :, rows] = _dot_nt(wvt_ref[...], xn).astype(BF16)
        for r in range(2):
            hi, lo = _split2(k_raw[r] * k_raw[r])
            mean_sq.append(_dot(jnp.concatenate([hi, lo], axis=1), sum_ref[...]) * (1.0 / HEAD_DIM))
        for r in range(2):
            rh, rl = _split2(lax.rsqrt(mean_sq[r] + EPS))
            inv = _dot(jnp.concatenate([rh, rl], axis=1), spread_ref[...])
            k_ref[0, r * half:(r + 1) * half, :] = (k_raw[r] * inv * kn_ref[...]).astype(BF16)


def _shared_kv(x, g, w_kv, k_norm):
    b, s, _ = x.shape
    tk = PAD_LEN
    hsum, hspread = _head_norm_constants()
    return pl.pallas_call(
        _kv_kernel,
        grid=(b, s // tk + 1),
        in_specs=[
            pl.BlockSpec((1, tk, D_MODEL), lambda i, j: (i, jnp.maximum(j - 1, 0), 0)),
            _resident((1, D_MODEL)),
            _resident((D_MODEL, ATT_DIM)),
            _resident((ATT_DIM, D_MODEL)),
            _resident((1, ATT_DIM)),
            _resident(hsum.shape),
            _resident(hspread.shape),
        ],
        out_specs=[pl.BlockSpec((1, tk, ATT_DIM), lambda i, j: (i, j, 0)),
                   pl.BlockSpec((1, ATT_DIM, tk), lambda i, j: (i, 0, j))],
        out_shape=[jax.ShapeDtypeStruct((b, s + PAD_LEN, ATT_DIM), BF16),
                   jax.ShapeDtypeStruct((b, ATT_DIM, s + PAD_LEN), BF16)],
        compiler_params=_params("arbitrary", "arbitrary"),
        name="shared_kv",
    )(x, g.reshape(1, D_MODEL), w_kv[:, :ATT_DIM].astype(BF16), w_kv[:, ATT_DIM:].T.astype(BF16),
      jnp.tile(k_norm, ATT_HEADS).reshape(1, ATT_DIM), hsum, hspread)


def _bias_kernel(w_ref, o_ref):
    rows = jnp.broadcast_to(w_ref[0], (NEAR_ROWS, TOEP_W))
    toep = pltpu.roll(rows, 0, 1, stride=1, stride_axis=0)[:, 0:ATT_ROWS]
    key_chunk = lax.broadcasted_iota(jnp.int32, (NEAR_ROWS, ATT_ROWS), 0) // CHUNK + (ATT_WIN - NEAR_ROWS) // CHUNK
    query_chunk = lax.broadcasted_iota(jnp.int32, (NEAR_ROWS, ATT_ROWS), 1) // CHUNK
    o_ref[0] = jnp.where(key_chunk - query_chunk <= LEFT_CHUNKS, toep, -jnp.inf)


def _rel_bias_tables(rel_bias):
    far = rel_bias[:, :1]
    w = jnp.concatenate([
        jnp.broadcast_to(far, (ATT_HEADS, ATT_ROWS + 1)),
        jnp.broadcast_to(rel_bias[:, -1:], (ATT_HEADS, TOEP_W - ATT_ROWS - 1 - 2 * MAX_REL)),
        rel_bias[:, :0:-1],
    ], axis=1).reshape(ATT_HEADS, 1, TOEP_W)
    near = pl.pallas_call(
        _bias_kernel,
        grid=(ATT_HEADS,),
        in_specs=[pl.BlockSpec((1, 1, TOEP_W), lambda h: (h, 0, 0))],
        out_specs=pl.BlockSpec((1, NEAR_ROWS, ATT_ROWS), lambda h: (h, 0, 0)),
        out_shape=jax.ShapeDtypeStruct((ATT_HEADS, NEAR_ROWS, ATT_ROWS), F32),
        compiler_params=_params("arbitrary"),
        name="rel_bias",
    )(w)
    return near, jnp.broadcast_to(far[:, :, None], (ATT_HEADS, 1, ATT_ROWS))


def _attn_layer_kernel(x_ref, g_ref, wq_ref, qn_ref, sum_ref, spread_ref, k_ref, vt_ref,
                       near_ref, far_ref, farmask_ref, wo_ref, o_ref, q_scr, ot_scr):
    base = pl.multiple_of(pl.program_id(1) * ATT_ROWS, ATT_ROWS)
    x = x_ref[0]
    xn = _rms(x, g_ref[...]).astype(BF16)

    def q_project(g):
        return _dot(xn, wq_ref[:, g * Q_COLS:(g + 1) * Q_COLS])

    def q_mean_sq(qraw):
        hi, lo = _split2(qraw * qraw)
        return _dot(jnp.concatenate([hi, lo], axis=1), sum_ref[...]) * (1.0 / HEAD_DIM)

    def q_inv_rms(ms):
        rh, rl = _split2(lax.rsqrt(ms + EPS))
        return _dot(jnp.concatenate([rh, rl], axis=1), spread_ref[...])

    def q_store(g, qraw, inv):
        cols = slice(g * Q_COLS, (g + 1) * Q_COLS)
        q_scr[:, cols] = (qraw * inv * qn_ref[:, cols]).astype(BF16)

    qraw = q_project(0)
    q_store(0, qraw, q_inv_rms(q_mean_sq(qraw)))

    lane = lax.broadcasted_iota(jnp.int32, (1, LANES), 1)
    head_mask = [(lane < HEAD_DIM).astype(BF16), (lane >= HEAD_DIM).astype(BF16)]
    n_kc = ATT_WIN // CHUNK
    far_kc = (ATT_WIN - NEAR_ROWS) // CHUNK
    pad_neg = [jnp.where(base + kc * CHUNK < PAD_LEN, -jnp.inf, 0.0).astype(F32)
               for kc in range(PAD_LEN // CHUNK)]

    chunks_per_half = LANES // CHUNK
    seen_kc = LEFT_CHUNKS + chunks_per_half

    def scores_t(h):
        l0 = (h // 2) * LANES
        q_pair = q_scr[:, pl.ds(l0, LANES)]
        k_win = k_ref[0, pl.ds(base, ATT_WIN), pl.ds(l0, LANES)]
        st = _dot_nt(k_win, q_pair * head_mask[h % 2])
        far_row = far_ref[h]
        halves = []
        for hf in range(ATT_ROWS // LANES):
            lanes = slice(hf * LANES, (hf + 1) * LANES)
            first_kc = hf * chunks_per_half
            blocks = []
            for kc in range(first_kc, first_kc + seen_kc):
                rows = slice(kc * CHUNK, (kc + 1) * CHUNK)
                blk = st[rows, lanes]
                if kc < far_kc:
                    blk = blk + (far_row[:, lanes] + pad_neg[kc])
                    if kc == first_kc:
                        blk = blk + farmask_ref[rows, lanes]
                else:
                    blk = blk + near_ref[h, (kc - far_kc) * CHUNK:(kc - far_kc + 1) * CHUNK, lanes]
                    if kc < len(pad_neg):
                        blk = blk + pad_neg[kc]
                blocks.append(blk)
            s = jnp.concatenate(blocks, axis=0)
            halves.append((s, jnp.max(s, axis=0, keepdims=True)))
        return halves

    heads_per_group = Q_COLS // HEAD_DIM
    half_dim = ATT_DIM // 2
    out = x
    qraw = q_project(1)
    pending = [scores_t(h) for h in range(SCORE_LOOKAHEAD)]
    for h in range(ATT_HEADS):
        halves = pending.pop(0)
        g_next, stage = divmod(h + SCORE_LOOKAHEAD + heads_per_group - 1, heads_per_group)
        if 1 <= g_next < ATT_DIM // Q_COLS:
            if stage == 0:
                qraw = q_project(g_next)
            elif stage == 1:
                ms = q_mean_sq(qraw)
            elif stage == 2:
                inv = q_inv_rms(ms)
            else:
                q_store(g_next, qraw, inv)
        if h + SCORE_LOOKAHEAD < ATT_HEADS:
            pending.append(scores_t(h + SCORE_LOOKAHEAD))
        p_cols, denoms = [], []
        for hf, (s, s_max) in enumerate(halves):
            p_half = jnp.exp2(s - s_max)
            denoms.append(jnp.sum(p_half, axis=0, keepdims=True))
            above = jnp.zeros((hf * chunks_per_half * CHUNK, LANES), BF16)
            below = jnp.zeros((ATT_WIN - (hf * chunks_per_half + seen_kc) * CHUNK, LANES), BF16)
            p_cols.append(jnp.concatenate(
                [blk for blk in (above, p_half.astype(BF16), below) if blk.shape[0]], axis=0))
        p = jnp.concatenate(p_cols, axis=1)
        denom = jnp.concatenate(denoms, axis=1)
        r0 = h * HEAD_DIM
        vt = vt_ref[0, pl.ds(r0, HEAD_DIM), pl.ds(base, ATT_WIN)]
        ot = _dot(vt, p) / denom
        ot_scr[pl.ds(r0, HEAD_DIM), :] = ot.astype(BF16)
        if h == ATT_HEADS // 2:
            out = out + _dot_tn(ot_scr[0:half_dim, :], wo_ref[0:half_dim, :])
    o_ref[0] = out + _dot_tn(ot_scr[half_dim:, :], wo_ref[half_dim:, :])


def _attn_layer(x, k_pad, vt_pad, g, w_q, q_norm, rel_bias, w_o):
    b, s, _ = x.shape
    hsum, hspread = _head_norm_constants(Q_COLS)
    log2e = 1.0 / math.log(2.0)
    near, far = _rel_bias_tables(rel_bias * log2e)
    key_chunk = jnp.arange(FAR_MASK_CHUNKS * CHUNK) // CHUNK
    query_chunk = jnp.arange(ATT_ROWS) // CHUNK
    farmask = jnp.where(key_chunk[:, None] >= query_chunk[None, :], 0.0, -jnp.inf).astype(F32)
    qn = jnp.tile(q_norm, ATT_HEADS).reshape(1, ATT_DIM) * (log2e / math.sqrt(HEAD_DIM))
    return pl.pallas_call(
        _attn_layer_kernel,
        grid=(b, s // ATT_ROWS),
        in_specs=[
            pl.BlockSpec((1, ATT_ROWS, D_MODEL), lambda i, j: (i, j, 0)),
            _resident((1, D_MODEL)),
            _resident((D_MODEL, ATT_DIM)),
            _resident((1, ATT_DIM)),
            _resident(hsum.shape),
            _resident(hspread.shape),
            pl.BlockSpec((1, s + PAD_LEN, ATT_DIM), lambda i, j: (i, 0, 0)),
            pl.BlockSpec((1, ATT_DIM, s + PAD_LEN), lambda i, j: (i, 0, 0)),
            _resident(near.shape),
            _resident(far.shape),
            _resident(farmask.shape),
            _resident((ATT_DIM, D_MODEL)),
        ],
        out_specs=pl.BlockSpec((1, ATT_ROWS, D_MODEL), lambda i, j: (i, j, 0)),
        out_shape=jax.ShapeDtypeStruct((b, s, D_MODEL), F32),
        scratch_shapes=[
            pltpu.VMEM((ATT_ROWS, ATT_DIM), BF16),
            pltpu.VMEM((ATT_DIM, ATT_ROWS), BF16),
        ],
        compiler_params=_params("arbitrary", "arbitrary"),
        name="attn_layer",
    )(x, g.reshape(1, D_MODEL), w_q.astype(BF16), qn, hsum, hspread, k_pad, vt_pad,
      near, far, farmask, w_o.astype(BF16))


def kernel(x, ffn1_norm, ffn1_w_gate, ffn1_w_up, ffn1_w_down, ffn2_norm, ffn2_w_gate, ffn2_w_up,
           ffn2_w_down, ssm_norm, ssm_in_proj, ssm_conv_w, ssm_conv_b, ssm_dt_bias, ssm_A_log, ssm_D,
           ssm_out_norm, ssm_out_proj, kv_norm, w_kv, k_norm, att_norm, att_w_q, att_q_norm,
           att_rel_bias, att_w_o):
    b, s, d = x.shape
    depth = ffn1_norm.shape[0]
    n_ssm = ssm_norm.shape[0]
    t = b * s
    k_pad = vt_pad = None
    ffn1 = [w.astype(BF16) for w in (ffn1_w_gate, ffn1_w_up, ffn1_w_down)]
    ffn2 = [w.astype(BF16) for w in (ffn2_w_gate, ffn2_w_up, ffn2_w_down)]
    ssm_in = ssm_in_proj.astype(BF16)
    ssm_out = ssm_out_proj.astype(BF16)
    for l in range(depth):
        x = _ffn(x.reshape(t, d), ffn1_norm[l], *ffn1, l).reshape(b, s, d)
        if l < n_ssm:
            x = _ssm(x, ssm_norm[l], ssm_in, ssm_in_proj[l, :, D_INNER + CONV_DIM:], ssm_conv_w[l],
                     ssm_conv_b[l], ssm_dt_bias[l], ssm_A_log[l], ssm_D[l], ssm_out_norm[l], ssm_out, l)
        else:
            if l == n_ssm:
                k_pad, vt_pad = _shared_kv(x, kv_norm, w_kv, k_norm)
            j = l - n_ssm
            x = _attn_layer(x, k_pad, vt_pad, att_norm[j], att_w_q[j], att_q_norm[j],
                            att_rel_bias[j], att_w_o[j])
        x = _ffn(x.reshape(t, d), ffn2_norm[l], *ffn2, l).reshape(b, s, d)
    return x
```
